```python
import math
import jax, jax.numpy as jnp
from jax import lax
import numpy as np

D_MODEL = 4096
BATCH = 1
SEQ = 8192
DEPTH = 1

CTX_LEN = 256
GRID_W = 64
D_CONV = 2048
CONV_WIDTH = 31
HEAD_DIM = 128
N_Q_HEADS = 16
N_KV_HEADS = 4
GROUP = N_Q_HEADS // N_KV_HEADS
D_ATTN = N_Q_HEADS * HEAD_DIM
D_KV = N_KV_HEADS * HEAD_DIM
D_IN = 2 * D_CONV + D_ATTN + 2 * D_KV
D_MIX = D_CONV + D_ATTN
D_FF = 11008
Q_BLOCK = 128
ROPE_THETA = 10000.0
ROPE_AXIS_DIM = HEAD_DIM // 2
EPS = 1e-6
N_MOD = 9
ATTN_SCALE = 1.0 / math.sqrt(HEAD_DIM)

kernel_name = "hymba_conformer_gqa_macaron_dit_block"


def _rms_norm(x, g):
    xf = x.astype(jnp.float32)
    y = xf * lax.rsqrt(jnp.mean(xf * xf, axis=-1, keepdims=True) + EPS)
    return (y * g.astype(jnp.float32)).astype(x.dtype)


def _layer_norm(x, g, b):
    xf = x.astype(jnp.float32)
    mu = jnp.mean(xf, axis=-1, keepdims=True)
    var = jnp.mean(jnp.square(xf - mu), axis=-1, keepdims=True)
    y = (xf - mu) * lax.rsqrt(var + EPS)
    return (y * g.astype(jnp.float32) + b.astype(jnp.float32)).astype(x.dtype)


def _modulate(h, shift, scale):
    return h * (1 + scale[:, None, :]) + shift[:, None, :]


def _swiglu(h, wg, wu, wd):
    return (jax.nn.silu(h @ wg) * (h @ wu)) @ wd


def _half_ffn(h, m, idx, g, wg, wu, wd):
    hn = _modulate(_rms_norm(h, g), m[:, idx], m[:, idx + 1])
    return h + 0.5 * m[:, idx + 2][:, None, :] * _swiglu(hn, wg, wu, wd)


def _axial_rope_tables(row, col):
    inv = ROPE_THETA ** (-jnp.arange(0, ROPE_AXIS_DIM, 2, dtype=jnp.float32) / ROPE_AXIS_DIM)
    ang = jnp.concatenate([row.astype(jnp.float32)[:, None] * inv,
                           col.astype(jnp.float32)[:, None] * inv], axis=-1)
    return jnp.cos(ang), jnp.sin(ang)


def _apply_rope(x, cos, sin):
    xf = x.astype(jnp.float32).reshape(*x.shape[:-1], HEAD_DIM // 2, 2)
    x0, x1 = xf[..., 0], xf[..., 1]
    c = cos[None, :, None, :]
    s = sin[None, :, None, :]
    out = jnp.stack([x0 * c - x1 * s, x0 * s + x1 * c], axis=-1).reshape(x.shape)
    return out.astype(x.dtype)


def _attend(q, keys, vals):
    B, Lq = q.shape[0], q.shape[1]
    qg = q.reshape(B, Lq, N_KV_HEADS, GROUP, HEAD_DIM)
    s = jnp.einsum('bqhgd,bkhd->bhgqk', qg, keys, preferred_element_type=jnp.float32) * ATTN_SCALE
    p = jax.nn.softmax(s, axis=-1).astype(vals.dtype)
    o = jnp.einsum('bhgqk,bkhd->bqhgd', p, vals)
    return o.reshape(B, Lq, D_ATTN)


def _attend_blocked(q, keys, vals):
    B, L = q.shape[0], q.shape[1]
    nb = L // Q_BLOCK
    qb = q.reshape(B, nb, Q_BLOCK, N_Q_HEADS, HEAD_DIM).swapaxes(0, 1)
    o = lax.map(lambda qblk: _attend(qblk, keys, vals), qb)
    return o.swapaxes(0, 1).reshape(B, L, D_ATTN)


def _conv_module(ug, w, b, ln_g, ln_b):
    u, g = ug[..., :D_CONV], ug[..., D_CONV:]
    glu = u * jax.nn.sigmoid(g)
    y = lax.conv_general_dilated(glu, w[:, None, :].astype(glu.dtype), window_strides=(1,),
                                 padding=[(CONV_WIDTH // 2, CONV_WIDTH // 2)],
                                 dimension_numbers=('NWC', 'WIO', 'NWC'),
                                 feature_group_count=D_CONV) + b
    return jax.nn.silu(_layer_norm(y, ln_g, ln_b))


def _split_qkv(p, qg, kg):
    B, L = p.shape[0], p.shape[1]
    o = 2 * D_CONV
    q = p[..., o:o + D_ATTN].reshape(B, L, N_Q_HEADS, HEAD_DIM)
    k = p[..., o + D_ATTN:o + D_ATTN + D_KV].reshape(B, L, N_KV_HEADS, HEAD_DIM)
    v = p[..., o + D_ATTN + D_KV:].reshape(B, L, N_KV_HEADS, HEAD_DIM)
    return _rms_norm(q, qg), _rms_norm(k, kg), v


def setup_inputs(seed: int = 0) -> dict:
    key = jax.random.key(seed)
    ks = jax.random.split(key, 24)
    f32 = jnp.float32

    def nrm(k, shape, fan_in, gain=1.0):
        return jax.random.normal(k, shape, f32) * (gain * fan_in ** -0.5)

    def ones_noise(k, shape):
        return 1.0 + 0.02 * jax.random.normal(k, shape, f32)

    def small(k, shape):
        return 0.01 * jax.random.normal(k, shape, f32)

    L = DEPTH
    return {
        "x": jax.random.normal(ks[0], (BATCH, SEQ, D_MODEL), f32),
        "c": jax.random.normal(ks[1], (BATCH, D_MODEL), f32),
        "ctx": jax.random.normal(ks[2], (BATCH, CTX_LEN, D_MODEL), f32),
        "c_ctx": jax.random.normal(ks[3], (D_MODEL,), f32),
        "w_ada": nrm(ks[4], (L, D_MODEL, N_MOD * D_MODEL), D_MODEL, 0.5),
        "b_ada": small(ks[5], (L, N_MOD * D_MODEL)),
        "g_ffn1": ones_noise(ks[6], (L, D_MODEL)),
        "w1_gate": nrm(ks[7], (L, D_MODEL, D_FF), D_MODEL),
        "w1_up": nrm(ks[8], (L, D_MODEL, D_FF), D_MODEL),
        "w1_down": nrm(ks[9], (L, D_FF, D_MODEL), D_FF),
        "g_mix": ones_noise(ks[10], (L, D_MODEL)),
        "w_in": nrm(ks[11], (L, D_MODEL, D_IN), D_MODEL),
        "conv_w": nrm(ks[12], (L, CONV_WIDTH, D_CONV), CONV_WIDTH),
        "conv_b": small(ks[13], (L, D_CONV)),
        "conv_ln_g": ones_noise(ks[14], (L, D_CONV)),
        "conv_ln_b": small(ks[15], (L, D_CONV)),
        "q_norm_g": ones_noise(ks[16], (L, HEAD_DIM)),
        "k_norm_g": ones_noise(ks[17], (L, HEAD_DIM)),
        "w_out": nrm(ks[18], (L, D_MIX, D_MODEL), D_MIX),
        "g_ffn2": ones_noise(ks[19], (L, D_MODEL)),
        "w2_gate": nrm(ks[20], (L, D_MODEL, D_FF), D_MODEL),
        "w2_up": nrm(ks[21], (L, D_MODEL, D_FF), D_MODEL),
        "w2_down": nrm(ks[22], (L, D_FF, D_MODEL), D_FF),
        "g_final": ones_noise(ks[23], (D_MODEL,)),
    }


def reference(x, c, ctx, c_ctx, w_ada, b_ada, g_ffn1, w1_gate, w1_up, w1_down, g_mix, w_in,
              conv_w, conv_b, conv_ln_g, conv_ln_b, q_norm_g, k_norm_g, w_out,
              g_ffn2, w2_gate, w2_up, w2_down, g_final):
    B, L, D = x.shape
    rows = L // GRID_W
    row = jnp.repeat(jnp.arange(rows, dtype=jnp.int32), GRID_W)
    col = jnp.tile(jnp.arange(GRID_W, dtype=jnp.int32), rows)
    cos, sin = _axial_rope_tables(row, col)

    cx = ctx
    for l in range(DEPTH):
        ctx_out = l < DEPTH - 1
        mx = (jax.nn.silu(c) @ w_ada[l] + b_ada[l]).reshape(B, N_MOD, D)
        mc = (jax.nn.silu(c_ctx) @ w_ada[l] + b_ada[l]).reshape(1, N_MOD, D)

        x = _half_ffn(x, mx, 0, g_ffn1[l], w1_gate[l], w1_up[l], w1_down[l])
        cx = _half_ffn(cx, mc, 0, g_ffn1[l], w1_gate[l], w1_up[l], w1_down[l])

        px = _modulate(_rms_norm(x, g_mix[l]), mx[:, 3], mx[:, 4]) @ w_in[l]
        pc = _modulate(_rms_norm(cx, g_mix[l]), mc[:, 3], mc[:, 4]) @ w_in[l]

        conv_x = _conv_module(px[..., :2 * D_CONV], conv_w[l], conv_b[l], conv_ln_g[l], conv_ln_b[l])

        qx, kx, vx = _split_qkv(px, q_norm_g[l], k_norm_g[l])
        qc, kc, vc = _split_qkv(pc, q_norm_g[l], k_norm_g[l])
        qx = _apply_rope(qx, cos, sin)
        kx = _apply_rope(kx, cos, sin)
        keys = jnp.concatenate([kx, kc], axis=1)
        vals = jnp.concatenate([vx, vc], axis=1)
        attn_x = _attend_blocked(qx, keys, vals)

        mix_x = jnp.concatenate([conv_x, attn_x], axis=-1) @ w_out[l]
        x = x + mx[:, 5][:, None, :] * mix_x

        if ctx_out:
            conv_c = _conv_module(pc[..., :2 * D_CONV], conv_w[l], conv_b[l], conv_ln_g[l], conv_ln_b[l])
            attn_c = _attend(qc, kc, vc)
            mix_c = jnp.concatenate([conv_c, attn_c], axis=-1) @ w_out[l]
            cx = cx + mc[:, 5][:, None, :] * mix_c
            cx = _half_ffn(cx, mc, 6, g_ffn2[l], w2_gate[l], w2_up[l], w2_down[l])

        x = _half_ffn(x, mx, 6, g_ffn2[l], w2_gate[l], w2_up[l], w2_down[l])

    return _rms_norm(x, g_final)
```

```python
import functools
import math

import jax
import jax.numpy as jnp
from jax import lax
from jax.experimental import pallas as pl
from jax.experimental.pallas import tpu as pltpu

F32 = jnp.float32
BF16 = jnp.bfloat16

EPS = 1e-6
HEAD_DIM = 128
N_Q_HEADS = 16
N_KV_HEADS = 4
GROUP = N_Q_HEADS // N_KV_HEADS
D_CONV = 2048
CONV_WIDTH = 31
CONV_HALO = 16
GRID_W = 64
ROPE_THETA = 10000.0
LOG2E = math.log2(math.e)
ATTN_SCALE = 1.0 / math.sqrt(HEAD_DIM)

V7X_VMEM_BYTES = 64 * 1024 * 1024
VMEM_LIMIT_BYTES = 58 * 1024 * 1024
LANES = 128

ROW_CHUNK = 64


def _params(semantics):
    return pltpu.CompilerParams(dimension_semantics=semantics, vmem_limit_bytes=VMEM_LIMIT_BYTES)


def _silu(v):
    return v * jax.nn.sigmoid(v)


def _rms_mod(x, g, shift, scale):
    ms = jnp.mean(x * x, axis=-1, keepdims=True)
    y = x * lax.rsqrt(ms + EPS) * g
    return y * (1.0 + scale) + shift


def _ada_kernel(c_ref, w_ref, b_ref, o_ref):
    a = _silu(c_ref[...]).astype(BF16)
    o_ref[...] = jnp.dot(a, w_ref[...].astype(BF16), preferred_element_type=F32) + b_ref[...]


def _ada(cc, w, b, tn=512):
    rows, d = cc.shape
    n = w.shape[1]
    return pl.pallas_call(
        _ada_kernel,
        grid=(n // tn,),
        in_specs=[
            pl.BlockSpec((rows, d), lambda j: (0, 0)),
            pl.BlockSpec((d, tn), lambda j: (0, j)),
            pl.BlockSpec((1, tn), lambda j: (0, j)),
        ],
        out_specs=pl.BlockSpec((rows, tn), lambda j: (0, j)),
        out_shape=jax.ShapeDtypeStruct((rows, n), F32),
        compiler_params=_params(("arbitrary",)),
        name="ada_mod",
    )(cc, w, b.reshape(1, n))


def _ffn_kernel(*refs, n_f, with_mix, final_norm):
    if with_mix:
        x_ref, mod_ref, g_ref, wg_ref, wu_ref, wd_ref, gmix_ref, out_ref, hmix_ref, hn_scr = refs
    elif final_norm:
        x_ref, mod_ref, g_ref, wg_ref, wu_ref, wd_ref, gfin_ref, out_ref, hn_scr = refs
    else:
        x_ref, mod_ref, g_ref, wg_ref, wu_ref, wd_ref, out_ref, hn_scr = refs
    f = pl.program_id(1)
    tm = x_ref.shape[0]
    n_chunks = tm // ROW_CHUNK

    @pl.when(f == 0)
    def _():
        def body(r, carry):
            rows = pl.ds(pl.multiple_of(r * ROW_CHUNK, ROW_CHUNK), ROW_CHUNK)
            hn = _rms_mod(x_ref[rows, :], g_ref[...], mod_ref[0:1, :], mod_ref[1:2, :])
            hn_scr[rows, :] = hn.astype(BF16)
            out_ref[rows, :] = jnp.zeros((ROW_CHUNK, out_ref.shape[1]), F32)
            return carry
        lax.fori_loop(0, n_chunks, body, 0)

    hn = hn_scr[...]
    gt = jnp.dot(hn, wg_ref[...], preferred_element_type=F32)
    up = jnp.dot(hn, wu_ref[...], preferred_element_type=F32)
    act = (_silu(gt) * up).astype(BF16)
    out_ref[...] += jnp.dot(act, wd_ref[...], preferred_element_type=F32)

    @pl.when(f == n_f - 1)
    def _():
        def body(r, carry):
            rows = pl.ds(pl.multiple_of(r * ROW_CHUNK, ROW_CHUNK), ROW_CHUNK)
            y = x_ref[rows, :] + (0.5 * mod_ref[2:3, :]) * out_ref[rows, :]
            if with_mix:
                hmix = _rms_mod(y, gmix_ref[...], mod_ref[3:4, :], mod_ref[4:5, :])
                hmix_ref[rows, :] = hmix.astype(BF16)
            if final_norm:
                ms = jnp.mean(y * y, axis=-1, keepdims=True)
                y = y * lax.rsqrt(ms + EPS) * gfin_ref[...]
            out_ref[rows, :] = y
            return carry
        lax.fori_loop(0, n_chunks, body, 0)


def _ffn(x, mod, g, wg, wu, wd, *, tm, tf=256, g_mix=None, g_final=None):
    m, d = x.shape
    tm = min(tm, m)
    dff = wg.shape[1]
    n_f = dff // tf
    with_mix = g_mix is not None
    final_norm = g_final is not None
    row = lambda i, f: (i, 0)
    const = lambda i, f: (0, 0)
    in_specs = [
        pl.BlockSpec((tm, d), row, pipeline_mode=pl.Buffered(1)),
        pl.BlockSpec(mod.shape, const),
        pl.BlockSpec((1, d), const),
        pl.BlockSpec((d, tf), lambda i, f: (0, f)),
        pl.BlockSpec((d, tf), lambda i, f: (0, f)),
        pl.BlockSpec((tf, d), lambda i, f: (f, 0)),
    ]
    args = [x, mod, g.reshape(1, d), wg, wu, wd]
    out_shape = [jax.ShapeDtypeStruct((m, d), F32)]
    out_specs = [pl.BlockSpec((tm, d), row)]
    if with_mix:
        in_specs.append(pl.BlockSpec((1, d), const))
        args.append(g_mix.reshape(1, d))
        out_shape.append(jax.ShapeDtypeStruct((m, d), BF16))
        out_specs.append(pl.BlockSpec((tm, d), row))
    if final_norm:
        in_specs.append(pl.BlockSpec((1, d), const))
        args.append(g_final.reshape(1, d))
    res = pl.pallas_call(
        functools.partial(_ffn_kernel, n_f=n_f, with_mix=with_mix, final_norm=final_norm),
        grid=(m // tm, n_f),
        in_specs=in_specs,
        out_specs=out_specs,
        out_shape=out_shape,
        scratch_shapes=[pltpu.VMEM((tm, d), BF16)],
        compiler_params=_params(("arbitrary", "arbitrary")),
        name="ffn_mix" if with_mix else ("ffn_final" if final_norm else "ffn"),
    )(*args)
    return res if with_mix else res[0]


def _glu_kernel(a_ref, wu_ref, wg_ref, o_ref):
    a = a_ref[...]
    u = jnp.dot(a, wu_ref[...], preferred_element_type=F32)
    g = jnp.dot(a, wg_ref[...], preferred_element_type=F32)
    o_ref[...] = u * jax.nn.sigmoid(g)


def _glu_proj(a, w, *, tm, tn=512):
    m, d = a.shape
    tm = min(tm, m)
    nb = D_CONV // tn
    return pl.pallas_call(
        _glu_kernel,
        grid=(m // tm, nb),
        in_specs=[
            pl.BlockSpec((tm, d), lambda i, j: (i, 0)),
            pl.BlockSpec((d, tn), lambda i, j: (0, j)),
            pl.BlockSpec((d, tn), lambda i, j: (0, j + nb)),
        ],
        out_specs=pl.BlockSpec((tm, tn), lambda i, j: (i, j)),
        out_shape=jax.ShapeDtypeStruct((m, D_CONV), F32),
        compiler_params=_params(("arbitrary", "arbitrary")),
        name="glu_proj",
    )(a, w, w)


def _qkv_kernel(a_ref, w_ref, gain_ref, cos_ref, sin_ref, o_ref, *, n_norm_tiles):
    j = pl.program_id(1)
    p = jnp.dot(a_ref[...], w_ref[...], preferred_element_type=F32)
    tn = p.shape[1]

    @pl.when(j < n_norm_tiles)
    def _():
        cosf = cos_ref[...]
        sinf = sin_ref[...]
        even = (lax.broadcasted_iota(jnp.int32, cosf.shape, 1) % 2) == 0
        for h in range(tn // HEAD_DIM):
            cols = slice(h * HEAD_DIM, (h + 1) * HEAD_DIM)
            ph = p[:, cols]
            ms = jnp.mean(ph * ph, axis=-1, keepdims=True)
            y = ph * lax.rsqrt(ms + EPS) * gain_ref[:, cols]
            partner = jnp.where(even, pltpu.roll(y, HEAD_DIM - 1, 1), pltpu.roll(y, 1, 1))
            o_ref[:, cols] = (y * cosf + partner * sinf).astype(o_ref.dtype)

    @pl.when(j >= n_norm_tiles)
    def _():
        o_ref[...] = p.astype(o_ref.dtype)


def _qkv_proj(a, w, gain, cosf, sinf, *, col0, n_tiles, n_norm_tiles, tm, tn=512):
    m, d = a.shape
    tm = min(tm, m)
    return pl.pallas_call(
        functools.partial(_qkv_kernel, n_norm_tiles=n_norm_tiles),
        grid=(m // tm, n_tiles),
        in_specs=[
            pl.BlockSpec((tm, d), lambda i, j: (i, 0)),
            pl.BlockSpec((d, tn), lambda i, j: (0, j + col0)),
            pl.BlockSpec((1, tn), lambda i, j: (0, j)),
            pl.BlockSpec((tm, HEAD_DIM), lambda i, j: (i, 0)),
            pl.BlockSpec((tm, HEAD_DIM), lambda i, j: (i, 0)),
        ],
        out_specs=pl.BlockSpec((tm, tn), lambda i, j: (i, j)),
        out_shape=jax.ShapeDtypeStruct((m, n_tiles * tn), BF16),
        compiler_params=_params(("arbitrary", "arbitrary")),
        name="qkv_proj",
    )(a, w, gain, cosf, sinf)


CONV_ROW_BLOCK = 32


def _conv_kernel(prev_ref, main_ref, next_ref, w_ref, b_ref, lng_ref, lnb_ref, o_ref, e_scr, y_scr):
    i = pl.program_id(0)
    n = pl.num_programs(0)
    ts, ch = main_ref.shape
    e_scr[0:CONV_HALO, :] = jnp.where(i > 0, prev_ref[...], 0.0)
    e_scr[CONV_HALO:CONV_HALO + ts, :] = main_ref[...]
    e_scr[CONV_HALO + ts:CONV_HALO + ts + CONV_HALO, :] = jnp.where(i < n - 1, next_ref[...], 0.0)
    first = CONV_HALO - CONV_WIDTH // 2

    def lane_tile(c, carry):
        cols = pl.ds(pl.multiple_of(c * LANES, LANES), LANES)
        taps = [w_ref[k:k + 1, cols] for k in range(CONV_WIDTH)]
        bias = b_ref[:, cols]
        for r in range(ts // CONV_ROW_BLOCK):
            r0 = r * CONV_ROW_BLOCK
            acc = jnp.zeros((CONV_ROW_BLOCK, LANES), F32)
            for k in range(CONV_WIDTH):
                acc = acc + e_scr[pl.ds(r0 + first + k, CONV_ROW_BLOCK), cols] * taps[k]
            y_scr[pl.ds(r0, CONV_ROW_BLOCK), cols] = acc + bias
        return carry
    lax.fori_loop(0, ch // LANES, lane_tile, 0)

    def ln_rows(r, carry):
        rows = pl.ds(pl.multiple_of(r * CONV_ROW_BLOCK, CONV_ROW_BLOCK), CONV_ROW_BLOCK)
        y = y_scr[rows, :]
        mu = jnp.mean(y, axis=-1, keepdims=True)
        yc = y - mu
        var = jnp.mean(yc * yc, axis=-1, keepdims=True)
        z = yc * lax.rsqrt(var + EPS) * lng_ref[...] + lnb_ref[...]
        o_ref[rows, :] = _silu(z).astype(o_ref.dtype)
        return carry
    lax.fori_loop(0, ts // CONV_ROW_BLOCK, ln_rows, 0)


def _conv_module(glu, w, b, ln_g, ln_b, *, ts=256):
    l, ch = glu.shape
    hb = ts // CONV_HALO
    n = l // ts
    last_halo_block = l // CONV_HALO - 1
    return pl.pallas_call(
        _conv_kernel,
        grid=(n,),
        in_specs=[
            pl.BlockSpec((CONV_HALO, ch), lambda i: (jnp.maximum(i * hb - 1, 0), 0)),
            pl.BlockSpec((ts, ch), lambda i: (i, 0)),
            pl.BlockSpec((CONV_HALO, ch), lambda i: (jnp.minimum((i + 1) * hb, last_halo_block), 0)),
            pl.BlockSpec((CONV_WIDTH, ch), lambda i: (0, 0)),
            pl.BlockSpec((1, ch), lambda i: (0, 0)),
            pl.BlockSpec((1, ch), lambda i: (0, 0)),
            pl.BlockSpec((1, ch), lambda i: (0, 0)),
        ],
        out_specs=pl.BlockSpec((ts, ch), lambda i: (i, 0)),
        out_shape=jax.ShapeDtypeStruct((l, ch), BF16),
        scratch_shapes=[pltpu.VMEM((ts + 2 * CONV_HALO, ch), F32), pltpu.VMEM((ts, ch), F32)],
        compiler_params=_params(("arbitrary",)),
        name="conv_module",
    )(glu, glu, glu, w, b.reshape(1, ch), ln_g.reshape(1, ch), ln_b.reshape(1, ch))


def _attn_kernel(q_ref, kx_ref, vx_ref, kc_ref, vc_ref, o_ref, qs_scr, m_scr, l_scr, acc_scr, *, tk):
    tq = q_ref.shape[0]
    for g in range(GROUP):
        qs_scr[g * tq:(g + 1) * tq, :] = q_ref[:, g * HEAD_DIM:(g + 1) * HEAD_DIM]
    m_scr[...] = jnp.full(m_scr.shape, -jnp.inf, F32)
    l_scr[...] = jnp.zeros(l_scr.shape, F32)
    acc_scr[...] = jnp.zeros(acc_scr.shape, F32)

    def step(k, v):
        s = lax.dot_general(qs_scr[...], k, (((1,), (1,)), ((), ())), preferred_element_type=F32)
        m_prev = m_scr[...]
        m_new = jnp.maximum(m_prev, jnp.max(s, axis=-1, keepdims=True))
        alpha = jnp.exp2(m_prev - m_new)
        p = jnp.exp2(s - m_new)
        l_scr[...] = alpha * l_scr[...] + jnp.sum(p, axis=-1, keepdims=True)
        acc_scr[...] = alpha * acc_scr[...] + jnp.dot(p.astype(BF16), v, preferred_element_type=F32)
        m_scr[...] = m_new

    def chunk(c, carry):
        rows = pl.ds(pl.multiple_of(c * tk, tk), tk)
        step(kx_ref[rows, :], vx_ref[rows, :])
        return carry
    lax.fori_loop(0, kx_ref.shape[0] // tk, chunk, 0)
    step(kc_ref[...], vc_ref[...])

    o = acc_scr[...] / l_scr[...]
    for g in range(GROUP):
        o_ref[:, g * HEAD_DIM:(g + 1) * HEAD_DIM] = o[g * tq:(g + 1) * tq, :].astype(o_ref.dtype)


def _attention(qkv_x, kv_c, *, tq=256, tk=1024):
    l = qkv_x.shape[0]
    c = kv_c.shape[0]
    tq, tk = min(tq, l), min(tk, l)
    gw = GROUP * HEAD_DIM
    k0 = N_Q_HEADS
    v0 = N_Q_HEADS + N_KV_HEADS
    m = GROUP * tq
    return pl.pallas_call(
        functools.partial(_attn_kernel, tk=tk),
        grid=(N_KV_HEADS, l // tq),
        in_specs=[
            pl.BlockSpec((tq, gw), lambda h, i: (i, h)),
            pl.BlockSpec((l, HEAD_DIM), lambda h, i: (0, k0 + h)),
            pl.BlockSpec((l, HEAD_DIM), lambda h, i: (0, v0 + h)),
            pl.BlockSpec((c, HEAD_DIM), lambda h, i: (0, h)),
            pl.BlockSpec((c, HEAD_DIM), lambda h, i: (0, N_KV_HEADS + h)),
        ],
        out_specs=pl.BlockSpec((tq, gw), lambda h, i: (i, h)),
        out_shape=jax.ShapeDtypeStruct((l, N_Q_HEADS * HEAD_DIM), BF16),
        scratch_shapes=[
            pltpu.VMEM((m, HEAD_DIM), BF16),
            pltpu.VMEM((m, 1), F32),
            pltpu.VMEM((m, 1), F32),
            pltpu.VMEM((m, HEAD_DIM), F32),
        ],
        compiler_params=_params(("arbitrary", "arbitrary")),
        name="gqa_attention",
    )(qkv_x, qkv_x, qkv_x, kv_c, kv_c)


def _outproj_kernel(conv_ref, attn_ref, wc_ref, wa_ref, x_ref, gate_ref, o_ref):
    mix = jnp.dot(conv_ref[...], wc_ref[...], preferred_element_type=F32)
    mix = mix + jnp.dot(attn_ref[...], wa_ref[...], preferred_element_type=F32)
    o_ref[...] = x_ref[...] + gate_ref[...] * mix


def _out_proj(conv_x, attn_x, w, x, gate, *, tm, tn=512):
    m, d = x.shape
    tm = min(tm, m)
    kc = conv_x.shape[1]
    ka = attn_x.shape[1]
    assert kc == ka
    return pl.pallas_call(
        _outproj_kernel,
        grid=(m // tm, d // tn),
        in_specs=[
            pl.BlockSpec((tm, kc), lambda i, j: (i, 0)),
            pl.BlockSpec((tm, ka), lambda i, j: (i, 0)),
            pl.BlockSpec((kc, tn), lambda i, j: (0, j)),
            pl.BlockSpec((ka, tn), lambda i, j: (1, j)),
            pl.BlockSpec((tm, tn), lambda i, j: (i, j)),
            pl.BlockSpec((1, tn), lambda i, j: (0, j)),
        ],
        out_specs=pl.BlockSpec((tm, tn), lambda i, j: (i, j)),
        out_shape=jax.ShapeDtypeStruct((m, d), F32),
        compiler_params=_params(("arbitrary", "arbitrary")),
        name="out_proj",
    )(conv_x, attn_x, w, w, x, gate)


def _rope_tables(l):
    pos = jnp.arange(l, dtype=jnp.int32)
    row = (pos // GRID_W).astype(F32)
    col = (pos % GRID_W).astype(F32)
    axis_dim = HEAD_DIM // 2
    inv = ROPE_THETA ** (-jnp.arange(0, axis_dim, 2, dtype=F32) / axis_dim)
    ang = jnp.concatenate([row[:, None] * inv, col[:, None] * inv], axis=-1)
    cos, sin = jnp.cos(ang), jnp.sin(ang)
    cosf = jnp.repeat(cos, 2, axis=-1)
    sinf = jnp.stack([-sin, sin], axis=-1).reshape(l, HEAD_DIM)
    return cosf, sinf


def kernel(x, c, ctx, c_ctx, w_ada, b_ada, g_ffn1, w1_gate, w1_up, w1_down, g_mix, w_in, conv_w, conv_b,
           conv_ln_g, conv_ln_b, q_norm_g, k_norm_g, w_out, g_ffn2, w2_gate, w2_up, w2_down, g_final):
    b, l, d = x.shape
    depth = w_ada.shape[0]
    assert b == 1 and depth == 1 and l % GRID_W == 0
    n_ctx = ctx.shape[1]
    xs = x[0]
    cs = ctx[0]

    cc = jnp.zeros((8, d), F32).at[0].set(c[0]).at[1].set(c_ctx)
    mods = _ada(cc, w_ada[0], b_ada[0])
    mx = mods[0].reshape(-1, d)
    mc = mods[1].reshape(-1, d)

    w1g, w1u, w1d = w1_gate[0].astype(BF16), w1_up[0].astype(BF16), w1_down[0].astype(BF16)
    w2g, w2u, w2d = w2_gate[0].astype(BF16), w2_up[0].astype(BF16), w2_down[0].astype(BF16)
    win = w_in[0].astype(BF16)
    wout = w_out[0].astype(BF16)

    x1, hx = _ffn(xs, mx[0:5], g_ffn1[0], w1g, w1u, w1d, tm=512, g_mix=g_mix[0])
    _, hc = _ffn(cs, mc[0:5], g_ffn1[0], w1g, w1u, w1d, tm=n_ctx, g_mix=g_mix[0])

    glu = _glu_proj(hx, win, tm=1024)
    tn = 512
    q_gain = jnp.tile(q_norm_g[0] * (ATTN_SCALE * LOG2E), N_Q_HEADS)
    k_gain = jnp.tile(k_norm_g[0], N_KV_HEADS)
    v_pad = jnp.ones((N_KV_HEADS * HEAD_DIM,), F32)
    cosf, sinf = _rope_tables(l)
    q_tiles = N_Q_HEADS * HEAD_DIM // tn
    k_tiles = N_KV_HEADS * HEAD_DIM // tn
    col_q = 2 * D_CONV // tn
    qkv_x = _qkv_proj(hx, win, jnp.concatenate([q_gain, k_gain, v_pad]).reshape(1, -1), cosf, sinf,
                      col0=col_q, n_tiles=q_tiles + 2 * k_tiles, n_norm_tiles=q_tiles + k_tiles, tm=1024, tn=tn)
    kv_c = _qkv_proj(hc, win, jnp.concatenate([k_gain, v_pad]).reshape(1, -1),
                     jnp.ones((n_ctx, HEAD_DIM), F32), jnp.zeros((n_ctx, HEAD_DIM), F32),
                     col0=col_q + q_tiles, n_tiles=2 * k_tiles, n_norm_tiles=k_tiles, tm=n_ctx, tn=tn)

    conv_x = _conv_module(glu, conv_w[0], conv_b[0], conv_ln_g[0], conv_ln_b[0])
    attn_x = _attention(qkv_x, kv_c)

    x2 = _out_proj(conv_x, attn_x, wout, x1, mx[5:6], tm=1024)

    out = _ffn(x2, mx[6:9], g_ffn2[0], w2g, w2u, w2d, tm=512, g_final=g_final)
    return out[None]
```

```python
import functools
import math

import jax
import jax.numpy as jnp
from jax import lax
from jax.experimental import pallas as pl
from jax.experimental.pallas import tpu as pltpu

F32 = jnp.float32
BF16 = jnp.bfloat16

EPS = 1e-6
HEAD_DIM = 128
N_Q_HEADS = 16
N_KV_HEADS = 4
GROUP = N_Q_HEADS // N_KV_HEADS
D_CONV = 2048
CONV_WIDTH = 31
CONV_HALO = 16
GRID_W = 64
ROPE_THETA = 10000.0
LOG2E = math.log2(math.e)
ATTN_SCALE = 1.0 / math.sqrt(HEAD_DIM)

V7X_VMEM_BYTES = 64 * 1024 * 1024
VMEM_LIMIT_BYTES = 58 * 1024 * 1024
LANES = 128

ROW_CHUNK = 64


def _params(semantics):
    return pltpu.CompilerParams(dimension_semantics=semantics, vmem_limit_bytes=VMEM_LIMIT_BYTES)


def _silu(v):
    return v * jax.nn.sigmoid(v)


def _rms_mod(x, g, shift, scale):
    ms = jnp.mean(x * x, axis=-1, keepdims=True)
    y = x * lax.rsqrt(ms + EPS) * g
    return y * (1.0 + scale) + shift


def _ada_kernel(c_ref, w_ref, b_ref, o_ref):
    a = _silu(c_ref[...]).astype(BF16)
    o_ref[...] = jnp.dot(a, w_ref[...].astype(BF16), preferred_element_type=F32) + b_ref[...]


def _ada(cc, w, b, tn=512):
    rows, d = cc.shape
    n = w.shape[1]
    return pl.pallas_call(
        _ada_kernel,
        grid=(n // tn,),
        in_specs=[
            pl.BlockSpec((rows, d), lambda j: (0, 0)),
            pl.BlockSpec((d, tn), lambda j: (0, j)),
            pl.BlockSpec((1, tn), lambda j: (0, j)),
        ],
        out_specs=pl.BlockSpec((rows, tn), lambda j: (0, j)),
        out_shape=jax.ShapeDtypeStruct((rows, n), F32),
        compiler_params=_params(("arbitrary",)),
        name="ada_mod",
    )(cc, w, b.reshape(1, n))


def _ffn_kernel(*refs, n_f, with_mix, final_norm):
    if with_mix:
        x_ref, mod_ref, g_ref, wg_ref, wu_ref, wd_ref, gmix_ref, out_ref, hmix_ref, hn_scr = refs
    elif final_norm:
        x_ref, mod_ref, g_ref, wg_ref, wu_ref, wd_ref, gfin_ref, out_ref, hn_scr = refs
    else:
        x_ref, mod_ref, g_ref, wg_ref, wu_ref, wd_ref, out_ref, hn_scr = refs
    f = pl.program_id(1)
    tm = x_ref.shape[0]
    n_chunks = tm // ROW_CHUNK

    @pl.when(f == 0)
    def _():
        def body(r, carry):
            rows = pl.ds(pl.multiple_of(r * ROW_CHUNK, ROW_CHUNK), ROW_CHUNK)
            hn = _rms_mod(x_ref[rows, :], g_ref[...], mod_ref[0:1, :], mod_ref[1:2, :])
            hn_scr[rows, :] = hn.astype(BF16)
            out_ref[rows, :] = jnp.zeros((ROW_CHUNK, out_ref.shape[1]), F32)
            return carry
        lax.fori_loop(0, n_chunks, body, 0)

    hn = hn_scr[...]
    gt = jnp.dot(hn, wg_ref[...], preferred_element_type=F32)
    up = jnp.dot(hn, wu_ref[...], preferred_element_type=F32)
    act = (_silu(gt) * up).astype(BF16)
    out_ref[...] += jnp.dot(act, wd_ref[...], preferred_element_type=F32)

    @pl.when(f == n_f - 1)
    def _():
        def body(r, carry):
            rows = pl.ds(pl.multiple_of(r * ROW_CHUNK, ROW_CHUNK), ROW_CHUNK)
            y = x_ref[rows, :] + (0.5 * mod_ref[2:3, :]) * out_ref[rows, :]
            if with_mix:
                hmix = _rms_mod(y, gmix_ref[...], mod_ref[3:4, :], mod_ref[4:5, :])
                hmix_ref[rows, :] = hmix.astype(BF16)
            if final_norm:
                ms = jnp.mean(y * y, axis=-1, keepdims=True)
                y = y * lax.rsqrt(ms + EPS) * gfin_ref[...]
            out_ref[rows, :] = y
            return carry
        lax.fori_loop(0, n_chunks, body, 0)


def _ffn(x, mod, g, wg, wu, wd, *, tm, tf=256, g_mix=None, g_final=None):
    m, d = x.shape
    tm = min(tm, m)
    dff = wg.shape[1]
    n_f = dff // tf
    with_mix = g_mix is not None
    final_norm = g_final is not None
    row = lambda i, f: (i, 0)
    const = lambda i, f: (0, 0)
    in_specs = [
        pl.BlockSpec((tm, d), row, pipeline_mode=pl.Buffered(1)),
        pl.BlockSpec(mod.shape, const),
        pl.BlockSpec((1, d), const),
        pl.BlockSpec((d, tf), lambda i, f: (0, f)),
        pl.BlockSpec((d, tf), lambda i, f: (0, f)),
        pl.BlockSpec((tf, d), lambda i, f: (f, 0)),
    ]
    args = [x, mod, g.reshape(1, d), wg, wu, wd]
    out_shape = [jax.ShapeDtypeStruct((m, d), F32)]
    out_specs = [pl.BlockSpec((tm, d), row)]
    if with_mix:
        in_specs.append(pl.BlockSpec((1, d), const))
        args.append(g_mix.reshape(1, d))
        out_shape.append(jax.ShapeDtypeStruct((m, d), BF16))
        out_specs.append(pl.BlockSpec((tm, d), row))
    if final_norm:
        in_specs.append(pl.BlockSpec((1, d), const))
        args.append(g_final.reshape(1, d))
    res = pl.pallas_call(
        functools.partial(_ffn_kernel, n_f=n_f, with_mix=with_mix, final_norm=final_norm),
        grid=(m // tm, n_f),
        in_specs=in_specs,
        out_specs=out_specs,
        out_shape=out_shape,
        scratch_shapes=[pltpu.VMEM((tm, d), BF16)],
        compiler_params=_params(("arbitrary", "arbitrary")),
        name="ffn_mix" if with_mix else ("ffn_final" if final_norm else "ffn"),
    )(*args)
    return res if with_mix else res[0]


def _glu_kernel(a_ref, wu_ref, wg_ref, o_ref):
    a = a_ref[...]
    u = jnp.dot(a, wu_ref[...], preferred_element_type=F32)
    g = jnp.dot(a, wg_ref[...], preferred_element_type=F32)
    o_ref[...] = u * jax.nn.sigmoid(g)


def _glu_proj(a, w, *, tm, tn=512):
    m, d = a.shape
    tm = min(tm, m)
    nb = D_CONV // tn
    return pl.pallas_call(
        _glu_kernel,
        grid=(m // tm, nb),
        in_specs=[
            pl.BlockSpec((tm, d), lambda i, j: (i, 0)),
            pl.BlockSpec((d, tn), lambda i, j: (0, j)),
            pl.BlockSpec((d, tn), lambda i, j: (0, j + nb)),
        ],
        out_specs=pl.BlockSpec((tm, tn), lambda i, j: (i, j)),
        out_shape=jax.ShapeDtypeStruct((m, D_CONV), F32),
        compiler_params=_params(("arbitrary", "arbitrary")),
        name="glu_proj",
    )(a, w, w)


def _qkv_kernel(a_ref, w_ref, gain_ref, cos_ref, sin_ref, o_ref, *, n_norm_tiles):
    j = pl.program_id(1)
    p = jnp.dot(a_ref[...], w_ref[...], preferred_element_type=F32)
    tn = p.shape[1]

    @pl.when(j < n_norm_tiles)
    def _():
        cosf = cos_ref[...]
        sinf = sin_ref[...]
        even = (lax.broadcasted_iota(jnp.int32, cosf.shape, 1) % 2) == 0
        for h in range(tn // HEAD_DIM):
            cols = slice(h * HEAD_DIM, (h + 1) * HEAD_DIM)
            ph = p[:, cols]
            ms = jnp.mean(ph * ph, axis=-1, keepdims=True)
            y = ph * lax.rsqrt(ms + EPS) * gain_ref[:, cols]
            partner = jnp.where(even, pltpu.roll(y, HEAD_DIM - 1, 1), pltpu.roll(y, 1, 1))
            o_ref[:, cols] = (y * cosf + partner * sinf).astype(o_ref.dtype)

    @pl.when(j >= n_norm_tiles)
    def _():
        o_ref[...] = p.astype(o_ref.dtype)


def _qkv_proj(a, w, gain, cosf, sinf, *, col0, n_tiles, n_norm_tiles, tm, tn=512, out_rows=None, into=None):
    m, d = a.shape
    tm = min(tm, m)
    in_specs = [
        pl.BlockSpec((tm, d), lambda i, j: (i, 0)),
        pl.BlockSpec((d, tn), lambda i, j: (0, j + col0)),
        pl.BlockSpec((1, tn), lambda i, j: (0, j)),
        pl.BlockSpec((tm, HEAD_DIM), lambda i, j: (i, 0)),
        pl.BlockSpec((tm, HEAD_DIM), lambda i, j: (i, 0)),
    ]
    args = [a, w, gain, cosf, sinf]
    kernel_fn = functools.partial(_qkv_kernel, n_norm_tiles=n_norm_tiles)
    if into is None:
        rows = m if out_rows is None else out_rows
        out_shape = jax.ShapeDtypeStruct((rows, n_tiles * tn), BF16)
        out_spec = pl.BlockSpec((tm, tn), lambda i, j: (i, j))
        aliases = {}
    else:
        dst, row0 = into
        assert row0 % tm == 0 and dst.shape[1] == n_tiles * tn
        blk0 = row0 // tm
        out_shape = jax.ShapeDtypeStruct(dst.shape, dst.dtype)
        out_spec = pl.BlockSpec((tm, tn), lambda i, j: (i + blk0, j))
        in_specs.append(pl.BlockSpec(memory_space=pl.ANY))
        args.append(dst)
        aliases = {len(args) - 1: 0}
        kernel_fn = functools.partial(_qkv_kernel_into, n_norm_tiles=n_norm_tiles)
    return pl.pallas_call(
        kernel_fn,
        grid=(m // tm, n_tiles),
        in_specs=in_specs,
        out_specs=out_spec,
        out_shape=out_shape,
        input_output_aliases=aliases,
        compiler_params=_params(("arbitrary", "arbitrary")),
        name="qkv_proj",
    )(*args)


def _qkv_kernel_into(a_ref, w_ref, gain_ref, cos_ref, sin_ref, dst_ref, o_ref, *, n_norm_tiles):
    del dst_ref
    _qkv_kernel(a_ref, w_ref, gain_ref, cos_ref, sin_ref, o_ref, n_norm_tiles=n_norm_tiles)


CONV_ROW_BLOCK = 32


def _conv_kernel(prev_ref, main_ref, next_ref, w_ref, b_ref, lng_ref, lnb_ref, o_ref, e_scr, y_scr):
    i = pl.program_id(0)
    n = pl.num_programs(0)
    ts, ch = main_ref.shape
    e_scr[0:CONV_HALO, :] = jnp.where(i > 0, prev_ref[...], 0.0)
    e_scr[CONV_HALO:CONV_HALO + ts, :] = main_ref[...]
    e_scr[CONV_HALO + ts:CONV_HALO + ts + CONV_HALO, :] = jnp.where(i < n - 1, next_ref[...], 0.0)
    first = CONV_HALO - CONV_WIDTH // 2

    def lane_tile(c, carry):
        cols = pl.ds(pl.multiple_of(c * LANES, LANES), LANES)
        taps = [w_ref[k:k + 1, cols] for k in range(CONV_WIDTH)]
        bias = b_ref[:, cols]
        for r in range(ts // CONV_ROW_BLOCK):
            r0 = r * CONV_ROW_BLOCK
            acc = jnp.zeros((CONV_ROW_BLOCK, LANES), F32)
            for k in range(CONV_WIDTH):
                acc = acc + e_scr[pl.ds(r0 + first + k, CONV_ROW_BLOCK), cols] * taps[k]
            y_scr[pl.ds(r0, CONV_ROW_BLOCK), cols] = acc + bias
        return carry
    lax.fori_loop(0, ch // LANES, lane_tile, 0)

    def ln_rows(r, carry):
        rows = pl.ds(pl.multiple_of(r * CONV_ROW_BLOCK, CONV_ROW_BLOCK), CONV_ROW_BLOCK)
        y = y_scr[rows, :]
        mu = jnp.mean(y, axis=-1, keepdims=True)
        yc = y - mu
        var = jnp.mean(yc * yc, axis=-1, keepdims=True)
        z = yc * lax.rsqrt(var + EPS) * lng_ref[...] + lnb_ref[...]
        o_ref[rows, :] = _silu(z).astype(o_ref.dtype)
        return carry
    lax.fori_loop(0, ts // CONV_ROW_BLOCK, ln_rows, 0)


def _conv_module(glu, w, b, ln_g, ln_b, *, ts=256):
    l, ch = glu.shape
    hb = ts // CONV_HALO
    n = l // ts
    last_halo_block = l // CONV_HALO - 1
    return pl.pallas_call(
        _conv_kernel,
        grid=(n,),
        in_specs=[
            pl.BlockSpec((CONV_HALO, ch), lambda i: (jnp.maximum(i * hb - 1, 0), 0)),
            pl.BlockSpec((ts, ch), lambda i: (i, 0)),
            pl.BlockSpec((CONV_HALO, ch), lambda i: (jnp.minimum((i + 1) * hb, last_halo_block), 0)),
            pl.BlockSpec((CONV_WIDTH, ch), lambda i: (0, 0)),
            pl.BlockSpec((1, ch), lambda i: (0, 0)),
            pl.BlockSpec((1, ch), lambda i: (0, 0)),
            pl.BlockSpec((1, ch), lambda i: (0, 0)),
        ],
        out_specs=pl.BlockSpec((ts, ch), lambda i: (i, 0)),
        out_shape=jax.ShapeDtypeStruct((l, ch), BF16),
        scratch_shapes=[pltpu.VMEM((ts + 2 * CONV_HALO, ch), F32), pltpu.VMEM((ts, ch), F32)],
        compiler_params=_params(("arbitrary",)),
        name="conv_module",
    )(glu, glu, glu, w, b.reshape(1, ch), ln_g.reshape(1, ch), ln_b.reshape(1, ch))


ATTN_ROW_SPLITS = 2
ATTN_MAX_KEY_CHUNK = 768
MXU_WIDTH = 256


def _attn_kernel(q_ref, k_ref, v_ref, o_ref, qs_scr, s0_scr, s1_scr, m_scr, l_scr, acc_scr, *, tk, n_chunks):
    tq = q_ref.shape[0]
    part = GROUP * tq // ATTN_ROW_SPLITS
    for g in range(GROUP):
        qs_scr[g * tq:(g + 1) * tq, :] = q_ref[:, g * HEAD_DIM:(g + 1) * HEAD_DIM]
    m_scr[...] = jnp.full(m_scr.shape, -jnp.inf, F32)
    l_scr[...] = jnp.zeros(l_scr.shape, F32)
    acc_scr[...] = jnp.zeros(acc_scr.shape, F32)

    def keys(c):
        return pl.ds(pl.multiple_of(c * tk, tk), tk)

    def scores(c, s_scr):
        k = k_ref[keys(c), :]
        for h in range(ATTN_ROW_SPLITS):
            r = slice(h * part, (h + 1) * part)
            s_scr[r, :] = lax.dot_general(qs_scr[r, :], k, (((1,), (1,)), ((), ())),
                                          preferred_element_type=F32)

    def accumulate(c, s_scr):
        v = v_ref[keys(c), :]
        for h in range(ATTN_ROW_SPLITS):
            r = slice(h * part, (h + 1) * part)
            s = s_scr[r, :]
            m_prev = m_scr[r, :]
            m_new = jnp.maximum(m_prev, jnp.max(s, axis=-1, keepdims=True))
            alpha = jnp.exp2(m_prev - m_new)
            p = jnp.exp2(s - m_new)
            l_scr[r, :] = alpha * l_scr[r, :] + jnp.sum(p, axis=-1, keepdims=True)
            acc_scr[r, :] = alpha * acc_scr[r, :] + jnp.dot(p.astype(BF16), v, preferred_element_type=F32)
            m_scr[r, :] = m_new

    scores(0, s0_scr)

    def pair(j, carry):
        scores(2 * j + 1, s1_scr)
        accumulate(2 * j, s0_scr)
        scores(2 * j + 2, s0_scr)
        accumulate(2 * j + 1, s1_scr)
        return carry
    lax.fori_loop(0, (n_chunks - 1) // 2, pair, 0)
    if n_chunks % 2 == 0:
        scores(n_chunks - 1, s1_scr)
        accumulate(n_chunks - 2, s0_scr)
        accumulate(n_chunks - 1, s1_scr)
    else:
        accumulate(n_chunks - 1, s0_scr)

    o = acc_scr[...] / l_scr[...]
    for g in range(GROUP):
        o_ref[:, g * HEAD_DIM:(g + 1) * HEAD_DIM] = o[g * tq:(g + 1) * tq, :].astype(o_ref.dtype)


def _key_chunk(n_keys):
    fits = [t for t in range(MXU_WIDTH, ATTN_MAX_KEY_CHUNK + 1, MXU_WIDTH) if n_keys % t == 0]
    assert fits, n_keys
    return fits[-1]


def _attention(q, kv, *, tq=256):
    l = q.shape[0]
    n_keys = kv.shape[0]
    tq = min(tq, l)
    tk = _key_chunk(n_keys)
    gw = GROUP * HEAD_DIM
    m = GROUP * tq
    return pl.pallas_call(
        functools.partial(_attn_kernel, tk=tk, n_chunks=n_keys // tk),
        grid=(N_KV_HEADS, l // tq),
        in_specs=[
            pl.BlockSpec((tq, gw), lambda h, i: (i, h)),
            pl.BlockSpec((n_keys, HEAD_DIM), lambda h, i: (0, h)),
            pl.BlockSpec((n_keys, HEAD_DIM), lambda h, i: (0, N_KV_HEADS + h)),
        ],
        out_specs=pl.BlockSpec((tq, gw), lambda h, i: (i, h)),
        out_shape=jax.ShapeDtypeStruct((l, N_Q_HEADS * HEAD_DIM), BF16),
        scratch_shapes=[
            pltpu.VMEM((m, HEAD_DIM), BF16),
            pltpu.VMEM((m, tk), F32),
            pltpu.VMEM((m, tk), F32),
            pltpu.VMEM((m, 1), F32),
            pltpu.VMEM((m, 1), F32),
            pltpu.VMEM((m, HEAD_DIM), F32),
        ],
        compiler_params=_params(("arbitrary", "arbitrary")),
        name="gqa_attention",
    )(q, kv, kv)


def _outproj_kernel(conv_ref, attn_ref, wc_ref, wa_ref, x_ref, gate_ref, o_ref):
    mix = jnp.dot(conv_ref[...], wc_ref[...], preferred_element_type=F32)
    mix = mix + jnp.dot(attn_ref[...], wa_ref[...], preferred_element_type=F32)
    o_ref[...] = x_ref[...] + gate_ref[...] * mix


def _out_proj(conv_x, attn_x, w, x, gate, *, tm, tn=512):
    m, d = x.shape
    tm = min(tm, m)
    kc = conv_x.shape[1]
    ka = attn_x.shape[1]
    assert kc == ka
    return pl.pallas_call(
        _outproj_kernel,
        grid=(m // tm, d // tn),
        in_specs=[
            pl.BlockSpec((tm, kc), lambda i, j: (i, 0)),
            pl.BlockSpec((tm, ka), lambda i, j: (i, 0)),
            pl.BlockSpec((kc, tn), lambda i, j: (0, j)),
            pl.BlockSpec((ka, tn), lambda i, j: (1, j)),
            pl.BlockSpec((tm, tn), lambda i, j: (i, j)),
            pl.BlockSpec((1, tn), lambda i, j: (0, j)),
        ],
        out_specs=pl.BlockSpec((tm, tn), lambda i, j: (i, j)),
        out_shape=jax.ShapeDtypeStruct((m, d), F32),
        compiler_params=_params(("arbitrary", "arbitrary")),
        name="out_proj",
    )(conv_x, attn_x, w, w, x, gate)


def _rope_tables(l):
    pos = jnp.arange(l, dtype=jnp.int32)
    row = (pos // GRID_W).astype(F32)
    col = (pos % GRID_W).astype(F32)
    axis_dim = HEAD_DIM // 2
    inv = ROPE_THETA ** (-jnp.arange(0, axis_dim, 2, dtype=F32) / axis_dim)
    ang = jnp.concatenate([row[:, None] * inv, col[:, None] * inv], axis=-1)
    cos, sin = jnp.cos(ang), jnp.sin(ang)
    cosf = jnp.repeat(cos, 2, axis=-1)
    sinf = jnp.stack([-sin, sin], axis=-1).reshape(l, HEAD_DIM)
    return cosf, sinf


def kernel(x, c, ctx, c_ctx, w_ada, b_ada, g_ffn1, w1_gate, w1_up, w1_down, g_mix, w_in, conv_w, conv_b,
           conv_ln_g, conv_ln_b, q_norm_g, k_norm_g, w_out, g_ffn2, w2_gate, w2_up, w2_down, g_final):
    b, l, d = x.shape
    depth = w_ada.shape[0]
    assert b == 1 and depth == 1 and l % GRID_W == 0
    n_ctx = ctx.shape[1]
    xs = x[0]
    cs = ctx[0]

    cc = jnp.zeros((8, d), F32).at[0].set(c[0]).at[1].set(c_ctx)
    mods = _ada(cc, w_ada[0], b_ada[0])
    mx = mods[0].reshape(-1, d)
    mc = mods[1].reshape(-1, d)

    w1g, w1u, w1d = w1_gate[0].astype(BF16), w1_up[0].astype(BF16), w1_down[0].astype(BF16)
    w2g, w2u, w2d = w2_gate[0].astype(BF16), w2_up[0].astype(BF16), w2_down[0].astype(BF16)
    win = w_in[0].astype(BF16)
    wout = w_out[0].astype(BF16)

    x1, hx = _ffn(xs, mx[0:5], g_ffn1[0], w1g, w1u, w1d, tm=512, g_mix=g_mix[0])
    _, hc = _ffn(cs, mc[0:5], g_ffn1[0], w1g, w1u, w1d, tm=n_ctx, g_mix=g_mix[0])

    glu = _glu_proj(hx, win, tm=1024)
    tn = 512
    q_gain = jnp.tile(q_norm_g[0] * (ATTN_SCALE * LOG2E), N_Q_HEADS)
    k_gain = jnp.tile(k_norm_g[0], N_KV_HEADS)
    v_pad = jnp.ones((N_KV_HEADS * HEAD_DIM,), F32)
    cosf, sinf = _rope_tables(l)
    q_tiles = N_Q_HEADS * HEAD_DIM // tn
    k_tiles = N_KV_HEADS * HEAD_DIM // tn
    col_q = 2 * D_CONV // tn
    kv_gain = jnp.concatenate([k_gain, v_pad]).reshape(1, -1)
    q = _qkv_proj(hx, win, q_gain.reshape(1, -1), cosf, sinf,
                  col0=col_q, n_tiles=q_tiles, n_norm_tiles=q_tiles, tm=1024, tn=tn)
    kv = _qkv_proj(hx, win, kv_gain, cosf, sinf, col0=col_q + q_tiles, n_tiles=2 * k_tiles,
                   n_norm_tiles=k_tiles, tm=1024, tn=tn, out_rows=l + n_ctx)
    kv = _qkv_proj(hc, win, kv_gain, jnp.ones((n_ctx, HEAD_DIM), F32), jnp.zeros((n_ctx, HEAD_DIM), F32),
                   col0=col_q + q_tiles, n_tiles=2 * k_tiles, n_norm_tiles=k_tiles, tm=n_ctx, tn=tn,
                   into=(kv, l))

    conv_x = _conv_module(glu, conv_w[0], conv_b[0], conv_ln_g[0], conv_ln_b[0])
    attn_x = _attention(q, kv)

    x2 = _out_proj(conv_x, attn_x, wout, x1, mx[5:6], tm=1024)

    out = _ffn(x2, mx[6:9], g_ffn2[0], w2g, w2u, w2d, tm=512, g_final=g_final)
    return out[None]
```

```python
import functools
import math

import jax
import jax.numpy as jnp
from jax import lax
from jax.experimental import pallas as pl
from jax.experimental.pallas import tpu as pltpu

F32 = jnp.float32
BF16 = jnp.bfloat16

EPS = 1e-6
HEAD_DIM = 128
N_Q_HEADS = 16
N_KV_HEADS = 4
GROUP = N_Q_HEADS // N_KV_HEADS
D_CONV = 2048
CONV_WIDTH = 31
GRID_W = 64
ROPE_THETA = 10000.0
LOG2E = math.log2(math.e)
ATTN_SCALE = 1.0 / math.sqrt(HEAD_DIM)

V7X_VMEM_BYTES = 64 * 1024 * 1024
VMEM_LIMIT_BYTES = V7X_VMEM_BYTES - 4 * 1024 * 1024
LANES = 128
SUBLANES = 8
MXU_WIDTH = 256

FFN_ROWS = 1024
FFN_COLS = MXU_WIDTH
FFN_ROW_CHUNK = 64
FFN_COL_BLOCK = 512
PROJ_ROWS = 1024
GLU_COLS = 256
QKV_COLS = 512
OUT_COLS = 512
CONV_ROWS = 256
CONV_HALO = 16
CONV_ROW_BLOCK = 32
LN_ROW_BLOCK = 128
ATTN_Q_ROWS = 256
ATTN_ROW_SPLITS = 2
ATTN_MAX_KEY_CHUNK = 768


def _params(semantics):
    return pltpu.CompilerParams(dimension_semantics=semantics, vmem_limit_bytes=VMEM_LIMIT_BYTES)


def _silu(v):
    return v * jax.nn.sigmoid(v)


def _bdot(a, w_ref):
    return jnp.dot(a, w_ref[...].astype(BF16), preferred_element_type=F32)


def _ada_kernel(c_ref, w_ref, b_ref, o_ref):
    a = _silu(c_ref[...]).astype(BF16)
    o_ref[...] = _bdot(a, w_ref) + b_ref[...]


def _ada(cc, w, b, tn=512):
    rows, d = cc.shape
    n = w.shape[1]
    return pl.pallas_call(
        _ada_kernel,
        grid=(n // tn,),
        in_specs=[
            pl.BlockSpec((rows, d), lambda j: (0, 0)),
            pl.BlockSpec((d, tn), lambda j: (0, j)),
            pl.BlockSpec((1, tn), lambda j: (0, j)),
        ],
        out_specs=pl.BlockSpec((rows, tn), lambda j: (0, j)),
        out_shape=jax.ShapeDtypeStruct((rows, n), F32),
        compiler_params=_params(("arbitrary",)),
        name="ada_mod",
    )(cc, w, b.reshape(1, n))


def _ffn_kernel(*refs, tm, n_f, write_out, with_mix, final_norm):
    refs = list(refs)
    x_hbm, mod_ref, g_ref, wg_ref, wu_ref, wd_ref = refs[:6]
    del refs[:6]
    gmix_ref = refs.pop(0) if with_mix else None
    gfin_ref = refs.pop(0) if final_norm else None
    out_hbm = refs.pop(0) if write_out else None
    hmix_hbm = refs.pop(0) if with_mix else None
    hn_scr, acc_scr, xbuf, sem_x, sem_o, sem_h = refs

    i = pl.program_id(0)
    f = pl.program_id(1)
    rc = FFN_ROW_CHUNK
    n_chunks = tm // rc
    row0 = i * tm

    def tile_rows(r):
        return pl.ds(pl.multiple_of(r * rc, rc), rc)

    def hbm_rows(r):
        return pl.ds(pl.multiple_of(row0 + r * rc, rc), rc)

    def x_copy(r, slot):
        return pltpu.make_async_copy(x_hbm.at[hbm_rows(r), :], xbuf.at[slot], sem_x.at[slot])

    def out_copy(r):
        return pltpu.make_async_copy(acc_scr.at[tile_rows(r), :], out_hbm.at[hbm_rows(r), :], sem_o.at[0])

    def hmix_copy(r):
        return pltpu.make_async_copy(hn_scr.at[tile_rows(r), :], hmix_hbm.at[hbm_rows(r), :], sem_h.at[0])

    def stream_x(process):
        x_copy(0, 0).start()

        def pair(j, carry):
            r = 2 * j
            x_copy(r + 1, 1).start()
            x_copy(r, 0).wait()
            process(r, 0)

            @pl.when(r + 2 < n_chunks)
            def _():
                x_copy(r + 2, 0).start()
            x_copy(r + 1, 1).wait()
            process(r + 1, 1)
            return carry
        lax.fori_loop(0, n_chunks // 2, pair, 0)

    d = acc_scr.shape[1]
    cw = FFN_COL_BLOCK
    col_blocks = [slice(c0, c0 + cw) for c0 in range(0, d, cw)]

    def add_sumsq(part, v):
        sq = v * v
        for t in range(cw // LANES):
            part = part + sq[:, t * LANES:(t + 1) * LANES]
        return part

    def rstd_of(part):
        return lax.rsqrt(jnp.sum(part, axis=-1, keepdims=True) / d + EPS)

    def norm_mod(v, rstd, gain_ref, shift_row, scale_row, cols):
        y = v * rstd * gain_ref[:, cols]
        return y * (1.0 + mod_ref[scale_row:scale_row + 1, cols]) + mod_ref[shift_row:shift_row + 1, cols]

    @pl.when(f == 0)
    def _():
        def prologue(r, slot):
            rows = tile_rows(r)
            part = jnp.zeros((rc, LANES), F32)
            for cols in col_blocks:
                part = add_sumsq(part, xbuf[slot, :, cols])
            rstd = rstd_of(part)
            for cols in col_blocks:
                hn = norm_mod(xbuf[slot, :, cols], rstd, g_ref, 0, 1, cols)
                hn_scr[rows, cols] = hn.astype(BF16)
                acc_scr[rows, cols] = jnp.zeros((rc, cw), F32)
        stream_x(prologue)

    hn = hn_scr[...]
    gt = _bdot(hn, wg_ref)
    up = _bdot(hn, wu_ref)
    act = (_silu(gt) * up).astype(BF16)
    acc_scr[...] += _bdot(act, wd_ref)

    @pl.when(f == n_f - 1)
    def _():
        def epilogue(r, slot):
            rows = tile_rows(r)
            part = jnp.zeros((rc, LANES), F32)
            for cols in col_blocks:
                y = xbuf[slot, :, cols] + (0.5 * mod_ref[2:3, cols]) * acc_scr[rows, cols]
                acc_scr[rows, cols] = y
                part = add_sumsq(part, y)
            rstd = rstd_of(part)
            for cols in col_blocks:
                y = acc_scr[rows, cols]
                if with_mix:
                    hn_scr[rows, cols] = norm_mod(y, rstd, gmix_ref, 3, 4, cols).astype(BF16)
                if final_norm:
                    acc_scr[rows, cols] = y * rstd * gfin_ref[:, cols]
            if with_mix:
                hmix_copy(r).start()
            if write_out:
                out_copy(r).start()
        stream_x(epilogue)
        for _ in range(n_chunks):
            if with_mix:
                hmix_copy(0).wait()
            if write_out:
                out_copy(0).wait()


def _ffn(x, mod, g, wg, wu, wd, *, g_mix=None, g_final=None, write_out=True):
    m, d = x.shape
    tm = min(FFN_ROWS, m)
    tf = FFN_COLS
    dff = wg.shape[1]
    assert m % tm == 0 and dff % tf == 0 and tm % (2 * FFN_ROW_CHUNK) == 0
    n_f = dff // tf
    with_mix = g_mix is not None
    final_norm = g_final is not None
    assert write_out or with_mix
    const = lambda i, f: (0, 0)
    hbm = pl.BlockSpec(memory_space=pl.ANY)
    in_specs = [
        hbm,
        pl.BlockSpec(mod.shape, const),
        pl.BlockSpec((1, d), const),
        pl.BlockSpec((d, tf), lambda i, f: (0, f)),
        pl.BlockSpec((d, tf), lambda i, f: (0, f)),
        pl.BlockSpec((tf, d), lambda i, f: (f, 0)),
    ]
    args = [x, mod, g.reshape(1, d), wg, wu, wd]
    if with_mix:
        in_specs.append(pl.BlockSpec((1, d), const))
        args.append(g_mix.reshape(1, d))
    if final_norm:
        in_specs.append(pl.BlockSpec((1, d), const))
        args.append(g_final.reshape(1, d))
    out_shape, out_specs = [], []
    if write_out:
        out_shape.append(jax.ShapeDtypeStruct((m, d), F32))
        out_specs.append(hbm)
    if with_mix:
        out_shape.append(jax.ShapeDtypeStruct((m, d), BF16))
        out_specs.append(hbm)
    res = pl.pallas_call(
        functools.partial(_ffn_kernel, tm=tm, n_f=n_f, write_out=write_out, with_mix=with_mix,
                          final_norm=final_norm),
        grid=(m // tm, n_f),
        in_specs=in_specs,
        out_specs=out_specs,
        out_shape=out_shape,
        scratch_shapes=[
            pltpu.VMEM((tm, d), BF16),
            pltpu.VMEM((tm, d), F32),
            pltpu.VMEM((2, FFN_ROW_CHUNK, d), F32),
            pltpu.SemaphoreType.DMA((2,)),
            pltpu.SemaphoreType.DMA((1,)),
            pltpu.SemaphoreType.DMA((1,)),
        ],
        compiler_params=_params(("arbitrary", "arbitrary")),
        name="ffn_mix" if with_mix else "ffn_final",
    )(*args)
    return res if len(res) > 1 else res[0]


def _glu_kernel(a_ref, wu_ref, wg_ref, o_ref):
    a = a_ref[...]
    u = _bdot(a, wu_ref)
    g = _bdot(a, wg_ref)
    o_ref[...] = u * jax.nn.sigmoid(g)


def _glu_proj(a, w):
    m, d = a.shape
    tm = min(PROJ_ROWS, m)
    tn = GLU_COLS
    nb = D_CONV // tn
    return pl.pallas_call(
        _glu_kernel,
        grid=(m // tm, nb),
        in_specs=[
            pl.BlockSpec((tm, d), lambda i, j: (i, 0)),
            pl.BlockSpec((d, tn), lambda i, j: (0, j)),
            pl.BlockSpec((d, tn), lambda i, j: (0, j + nb)),
        ],
        out_specs=pl.BlockSpec((tm, tn), lambda i, j: (i, j)),
        out_shape=jax.ShapeDtypeStruct((m, D_CONV), F32),
        compiler_params=_params(("arbitrary", "arbitrary")),
        name="glu_proj",
    )(a, w, w)


def _qkv_kernel(a_ref, w_ref, gain_ref, cos_ref, sin_ref, o_ref, *, n_norm_tiles):
    j = pl.program_id(1)
    p = _bdot(a_ref[...], w_ref)
    tn = p.shape[1]

    @pl.when(j < n_norm_tiles)
    def _():
        cosf = cos_ref[...]
        sinf = sin_ref[...]
        even = (lax.broadcasted_iota(jnp.int32, cosf.shape, 1) % 2) == 0
        for h in range(tn // HEAD_DIM):
            cols = slice(h * HEAD_DIM, (h + 1) * HEAD_DIM)
            ph = p[:, cols]
            ms = jnp.mean(ph * ph, axis=-1, keepdims=True)
            y = ph * lax.rsqrt(ms + EPS) * gain_ref[:, cols]
            partner = jnp.where(even, pltpu.roll(y, HEAD_DIM - 1, 1), pltpu.roll(y, 1, 1))
            o_ref[:, cols] = (y * cosf + partner * sinf).astype(o_ref.dtype)

    @pl.when(j >= n_norm_tiles)
    def _():
        o_ref[...] = p.astype(o_ref.dtype)


def _qkv_proj(a, w, gain, cosf, sinf, *, col0, n_tiles, n_norm_tiles):
    m, d = a.shape
    tm = min(PROJ_ROWS, m)
    tn = QKV_COLS
    return pl.pallas_call(
        functools.partial(_qkv_kernel, n_norm_tiles=n_norm_tiles),
        grid=(m // tm, n_tiles),
        in_specs=[
            pl.BlockSpec((tm, d), lambda i, j: (i, 0)),
            pl.BlockSpec((d, tn), lambda i, j: (0, j + col0)),
            pl.BlockSpec((1, tn), lambda i, j: (0, j)),
            pl.BlockSpec((tm, HEAD_DIM), lambda i, j: (i, 0)),
            pl.BlockSpec((tm, HEAD_DIM), lambda i, j: (i, 0)),
        ],
        out_specs=pl.BlockSpec((tm, tn), lambda i, j: (i, j)),
        out_shape=jax.ShapeDtypeStruct((m, n_tiles * tn), BF16),
        compiler_params=_params(("arbitrary", "arbitrary")),
        name="qkv_proj",
    )(a, w, gain, cosf, sinf)


def _conv_kernel(prev_ref, main_ref, next_ref, w_ref, b_ref, lng_ref, lnb_ref, o_ref, e_scr, ph_scr, y_scr):
    i = pl.program_id(0)
    n = pl.num_programs(0)
    ts, ch = main_ref.shape
    e_scr[0:CONV_HALO, :] = jnp.where(i > 0, prev_ref[...], 0.0)
    e_scr[CONV_HALO:CONV_HALO + ts, :] = main_ref[...]
    e_scr[CONV_HALO + ts:CONV_HALO + ts + CONV_HALO, :] = jnp.where(i < n - 1, next_ref[...], 0.0)
    first = CONV_HALO - CONV_WIDTH // 2
    ph_rows = ph_scr.shape[1]

    def lane_tile(c, carry):
        cols = pl.ds(pl.multiple_of(c * LANES, LANES), LANES)
        for ph in range(SUBLANES):
            ph_scr[ph] = e_scr[pl.ds(ph, ph_rows), cols]
        taps = [w_ref[k:k + 1, cols] for k in range(CONV_WIDTH)]
        bias = b_ref[:, cols]
        for r in range(ts // CONV_ROW_BLOCK):
            r0 = r * CONV_ROW_BLOCK
            acc = jnp.zeros((CONV_ROW_BLOCK, LANES), F32)
            for k in range(CONV_WIDTH):
                off = first + k
                acc = acc + ph_scr[off % SUBLANES, pl.ds(r0 + off - off % SUBLANES, CONV_ROW_BLOCK), :] * taps[k]
            y_scr[pl.ds(r0, CONV_ROW_BLOCK), cols] = acc + bias
        return carry
    lax.fori_loop(0, ch // LANES, lane_tile, 0)

    lb = min(LN_ROW_BLOCK, ts)

    def ln_rows(r, carry):
        rows = pl.ds(pl.multiple_of(r * lb, lb), lb)
        y = y_scr[rows, :]
        mu = jnp.mean(y, axis=-1, keepdims=True)
        yc = y - mu
        var = jnp.mean(yc * yc, axis=-1, keepdims=True)
        z = yc * lax.rsqrt(var + EPS) * lng_ref[...] + lnb_ref[...]
        o_ref[rows, :] = _silu(z).astype(o_ref.dtype)
        return carry
    lax.fori_loop(0, ts // lb, ln_rows, 0)


def _conv_module(glu, w, b, ln_g, ln_b):
    l, ch = glu.shape
    ts = min(CONV_ROWS, l)
    hb = ts // CONV_HALO
    n = l // ts
    last_halo_block = l // CONV_HALO - 1
    ph_rows = ts + 2 * CONV_HALO - SUBLANES
    assert CONV_HALO - CONV_WIDTH // 2 + CONV_WIDTH - 1 + ts <= ph_rows + SUBLANES - 1
    return pl.pallas_call(
        _conv_kernel,
        grid=(n,),
        in_specs=[
            pl.BlockSpec((CONV_HALO, ch), lambda i: (jnp.maximum(i * hb - 1, 0), 0)),
            pl.BlockSpec((ts, ch), lambda i: (i, 0)),
            pl.BlockSpec((CONV_HALO, ch), lambda i: (jnp.minimum((i + 1) * hb, last_halo_block), 0)),
            pl.BlockSpec((CONV_WIDTH, ch), lambda i: (0, 0)),
            pl.BlockSpec((1, ch), lambda i: (0, 0)),
            pl.BlockSpec((1, ch), lambda i: (0, 0)),
            pl.BlockSpec((1, ch), lambda i: (0, 0)),
        ],
        out_specs=pl.BlockSpec((ts, ch), lambda i: (i, 0)),
        out_shape=jax.ShapeDtypeStruct((l, ch), BF16),
        scratch_shapes=[
            pltpu.VMEM((ts + 2 * CONV_HALO, ch), F32),
            pltpu.VMEM((SUBLANES, ph_rows, LANES), F32),
            pltpu.VMEM((ts, ch), F32),
        ],
        compiler_params=_params(("arbitrary",)),
        name="conv_module",
    )(glu, glu, glu, w, b.reshape(1, ch), ln_g.reshape(1, ch), ln_b.reshape(1, ch))


def _attn_kernel(q_ref, k_ref, v_ref, o_ref, qs_scr, s0_scr, s1_scr, m_scr, l_scr, acc_scr, *, tk, n_chunks):
    tq = q_ref.shape[0]
    part = GROUP * tq // ATTN_ROW_SPLITS
    for g in range(GROUP):
        qs_scr[g * tq:(g + 1) * tq, :] = q_ref[:, g * HEAD_DIM:(g + 1) * HEAD_DIM]
    m_scr[...] = jnp.full(m_scr.shape, -jnp.inf, F32)
    l_scr[...] = jnp.zeros(l_scr.shape, F32)
    acc_scr[...] = jnp.zeros(acc_scr.shape, F32)

    def keys(c):
        return pl.ds(pl.multiple_of(c * tk, tk), tk)

    def scores(c, s_scr):
        k = k_ref[keys(c), :]
        for h in range(ATTN_ROW_SPLITS):
            r = slice(h * part, (h + 1) * part)
            s_scr[r, :] = lax.dot_general(qs_scr[r, :], k, (((1,), (1,)), ((), ())),
                                          preferred_element_type=F32)

    def accumulate(c, s_scr):
        v = v_ref[keys(c), :]
        for h in range(ATTN_ROW_SPLITS):
            r = slice(h * part, (h + 1) * part)
            s = s_scr[r, :]
            m_prev = m_scr[r, :]
            m_new = jnp.maximum(m_prev, jnp.max(s, axis=-1, keepdims=True))
            alpha = jnp.exp2(m_prev - m_new)
            p = jnp.exp2(s - jnp.tile(m_new, (1, tk // LANES)))
            l_scr[r, :] = alpha * l_scr[r, :] + jnp.sum(p, axis=-1, keepdims=True)
            acc_scr[r, :] = alpha * acc_scr[r, :] + jnp.dot(p.astype(BF16), v, preferred_element_type=F32)
            m_scr[r, :] = m_new

    scores(0, s0_scr)

    def pair(j, carry):
        scores(2 * j + 1, s1_scr)
        accumulate(2 * j, s0_scr)
        scores(2 * j + 2, s0_scr)
        accumulate(2 * j + 1, s1_scr)
        return carry
    lax.fori_loop(0, (n_chunks - 1) // 2, pair, 0)
    if n_chunks % 2 == 0:
        scores(n_chunks - 1, s1_scr)
        accumulate(n_chunks - 2, s0_scr)
        accumulate(n_chunks - 1, s1_scr)
    else:
        accumulate(n_chunks - 1, s0_scr)

    o = acc_scr[...] / l_scr[...]
    for g in range(GROUP):
        o_ref[:, g * HEAD_DIM:(g + 1) * HEAD_DIM] = o[g * tq:(g + 1) * tq, :].astype(o_ref.dtype)


def _key_chunk(n_keys):
    fits = [t for t in range(MXU_WIDTH, ATTN_MAX_KEY_CHUNK + 1, MXU_WIDTH) if n_keys % t == 0]
    assert fits, n_keys
    return fits[-1]


def _attention(q, kv):
    l = q.shape[0]
    n_keys = kv.shape[0]
    tq = min(ATTN_Q_ROWS, l)
    tk = _key_chunk(n_keys)
    gw = GROUP * HEAD_DIM
    m = GROUP * tq
    return pl.pallas_call(
        functools.partial(_attn_kernel, tk=tk, n_chunks=n_keys // tk),
        grid=(N_KV_HEADS, l // tq),
        in_specs=[
            pl.BlockSpec((tq, gw), lambda h, i: (i, h)),
            pl.BlockSpec((n_keys, HEAD_DIM), lambda h, i: (0, h)),
            pl.BlockSpec((n_keys, HEAD_DIM), lambda h, i: (0, N_KV_HEADS + h)),
        ],
        out_specs=pl.BlockSpec((tq, gw), lambda h, i: (i, h)),
        out_shape=jax.ShapeDtypeStruct((l, N_Q_HEADS * HEAD_DIM), BF16),
        scratch_shapes=[
            pltpu.VMEM((m, HEAD_DIM), BF16),
            pltpu.VMEM((m, tk), F32),
            pltpu.VMEM((m, tk), F32),
            pltpu.VMEM((m, LANES), F32),
            pltpu.VMEM((m, LANES), F32),
            pltpu.VMEM((m, HEAD_DIM), F32),
        ],
        compiler_params=_params(("arbitrary", "arbitrary")),
        name="gqa_attention",
    )(q, kv, kv)


def _outproj_kernel(conv_ref, attn_ref, wc_ref, wa_ref, x_ref, gate_ref, o_ref):
    mix = _bdot(conv_ref[...], wc_ref) + _bdot(attn_ref[...], wa_ref)
    o_ref[...] = x_ref[...] + gate_ref[...] * mix


def _out_proj(conv_x, attn_x, w, x, gate):
    m, d = x.shape
    tm = min(PROJ_ROWS, m)
    tn = OUT_COLS
    kc = conv_x.shape[1]
    ka = attn_x.shape[1]
    assert kc == ka
    return pl.pallas_call(
        _outproj_kernel,
        grid=(m // tm, d // tn),
        in_specs=[
            pl.BlockSpec((tm, kc), lambda i, j: (i, 0)),
            pl.BlockSpec((tm, ka), lambda i, j: (i, 0)),
            pl.BlockSpec((kc, tn), lambda i, j: (0, j)),
            pl.BlockSpec((ka, tn), lambda i, j: (1, j)),
            pl.BlockSpec((tm, tn), lambda i, j: (i, j)),
            pl.BlockSpec((1, tn), lambda i, j: (0, j)),
        ],
        out_specs=pl.BlockSpec((tm, tn), lambda i, j: (i, j)),
        out_shape=jax.ShapeDtypeStruct((m, d), F32),
        compiler_params=_params(("arbitrary", "arbitrary")),
        name="out_proj",
    )(conv_x, attn_x, w, w, x, gate)


def _rope_tables(l):
    pos = jnp.arange(l, dtype=jnp.int32)
    row = (pos // GRID_W).astype(F32)
    col = (pos % GRID_W).astype(F32)
    axis_dim = HEAD_DIM // 2
    inv = ROPE_THETA ** (-jnp.arange(0, axis_dim, 2, dtype=F32) / axis_dim)
    ang = jnp.concatenate([row[:, None] * inv, col[:, None] * inv], axis=-1)
    cos, sin = jnp.cos(ang), jnp.sin(ang)
    cosf = jnp.repeat(cos, 2, axis=-1)
    sinf = jnp.stack([-sin, sin], axis=-1).reshape(l, HEAD_DIM)
    return cosf, sinf


def kernel(x, c, ctx, c_ctx, w_ada, b_ada, g_ffn1, w1_gate, w1_up, w1_down, g_mix, w_in, conv_w, conv_b,
           conv_ln_g, conv_ln_b, q_norm_g, k_norm_g, w_out, g_ffn2, w2_gate, w2_up, w2_down, g_final):
    b, l, d = x.shape
    depth = w_ada.shape[0]
    assert b == 1 and depth == 1 and l % GRID_W == 0
    n_ctx = ctx.shape[1]
    xs = x[0]
    cs = ctx[0]

    cc = jnp.zeros((SUBLANES, d), F32).at[0].set(c[0]).at[1].set(c_ctx)
    mods = _ada(cc, w_ada[0], b_ada[0])
    mx = mods[0].reshape(-1, d)
    mc = mods[1].reshape(-1, d)

    x1, hx = _ffn(xs, mx[0:5], g_ffn1[0], w1_gate[0], w1_up[0], w1_down[0], g_mix=g_mix[0])
    hc = _ffn(cs, mc[0:5], g_ffn1[0], w1_gate[0], w1_up[0], w1_down[0], g_mix=g_mix[0], write_out=False)

    win = w_in[0]
    glu = _glu_proj(hx, win)
    q_gain = jnp.tile(q_norm_g[0] * (ATTN_SCALE * LOG2E), N_Q_HEADS)
    k_gain = jnp.tile(k_norm_g[0], N_KV_HEADS)
    v_pad = jnp.ones((N_KV_HEADS * HEAD_DIM,), F32)
    kv_gain = jnp.concatenate([k_gain, v_pad]).reshape(1, -1)
    cosf, sinf = _rope_tables(l)
    q_tiles = N_Q_HEADS * HEAD_DIM // QKV_COLS
    k_tiles = N_KV_HEADS * HEAD_DIM // QKV_COLS
    col_q = 2 * D_CONV // QKV_COLS
    q = _qkv_proj(hx, win, q_gain.reshape(1, -1), cosf, sinf, col0=col_q, n_tiles=q_tiles, n_norm_tiles=q_tiles)
    kv_x = _qkv_proj(hx, win, kv_gain, cosf, sinf, col0=col_q + q_tiles, n_tiles=2 * k_tiles,
                     n_norm_tiles=k_tiles)
    kv_c = _qkv_proj(hc, win, kv_gain, jnp.ones((n_ctx, HEAD_DIM), F32), jnp.zeros((n_ctx, HEAD_DIM), F32),
                     col0=col_q + q_tiles, n_tiles=2 * k_tiles, n_norm_tiles=k_tiles)
    kv = jnp.concatenate([kv_x, kv_c], axis=0)

    conv_x = _conv_module(glu, conv_w[0], conv_b[0], conv_ln_g[0], conv_ln_b[0])
    attn_x = _attention(q, kv)

    x2 = _out_proj(conv_x, attn_x, w_out[0], x1, mx[5:6])

    out = _ffn(x2, mx[6:9], g_ffn2[0], w2_gate[0], w2_up[0], w2_down[0], g_final=g_final)
    return out[None]
```

```python
import functools
import math

import jax
import jax.numpy as jnp
from jax import lax
from jax.experimental import pallas as pl
from jax.experimental.pallas import tpu as pltpu

F32 = jnp.float32
BF16 = jnp.bfloat16

EPS = 1e-6
HEAD_DIM = 128
N_Q_HEADS = 16
N_KV_HEADS = 4
GROUP = N_Q_HEADS // N_KV_HEADS
D_CONV = 2048
CONV_WIDTH = 31
GRID_W = 64
ROPE_THETA = 10000.0
LOG2E = math.log2(math.e)
ATTN_SCALE = 1.0 / math.sqrt(HEAD_DIM)

V7X_VMEM_BYTES = 64 * 1024 * 1024
VMEM_LIMIT_BYTES = V7X_VMEM_BYTES - 4 * 1024 * 1024
LANES = 128
SUBLANES = 8
MXU_WIDTH = 256

FFN_ROWS = 1024
FFN_COLS = MXU_WIDTH
FFN_ROW_CHUNK = 64
FFN_COL_BLOCK = 512
PROJ_ROWS = 1024
GLU_COLS = 256
QKV_COLS = 512
OUT_COLS = 512
CONV_ROWS = 256
CONV_HALO = 16
CONV_ROW_BLOCK = 32
LN_ROW_BLOCK = 128
ATTN_Q_ROWS = 256
ATTN_ROW_SPLITS = 2
ATTN_MAX_KEY_CHUNK = 1408


def _params(semantics):
    return pltpu.CompilerParams(dimension_semantics=semantics, vmem_limit_bytes=VMEM_LIMIT_BYTES)


def _silu(v):
    return v * jax.nn.sigmoid(v)


def _bdot(a, w_ref):
    return jnp.dot(a, w_ref[...].astype(BF16), preferred_element_type=F32)


def _ada_kernel(c_ref, w_ref, b_ref, o_ref):
    a = _silu(c_ref[...]).astype(BF16)
    o_ref[...] = _bdot(a, w_ref) + b_ref[...]


def _ada(cc, w, b, tn=512):
    rows, d = cc.shape
    n = w.shape[1]
    return pl.pallas_call(
        _ada_kernel,
        grid=(n // tn,),
        in_specs=[
            pl.BlockSpec((rows, d), lambda j: (0, 0)),
            pl.BlockSpec((d, tn), lambda j: (0, j)),
            pl.BlockSpec((1, tn), lambda j: (0, j)),
        ],
        out_specs=pl.BlockSpec((rows, tn), lambda j: (0, j)),
        out_shape=jax.ShapeDtypeStruct((rows, n), F32),
        compiler_params=_params(("arbitrary",)),
        name="ada_mod",
    )(cc, w, b.reshape(1, n))


def _ffn_kernel(*refs, tm, n_f, write_out, with_mix, final_norm):
    refs = list(refs)
    x_hbm, mod_ref, g_ref, wg_ref, wu_ref, wd_ref = refs[:6]
    del refs[:6]
    gmix_ref = refs.pop(0) if with_mix else None
    gfin_ref = refs.pop(0) if final_norm else None
    out_hbm = refs.pop(0) if write_out else None
    hmix_hbm = refs.pop(0) if with_mix else None
    hn_scr, acc_scr, xbuf, sem_x, sem_o, sem_h = refs

    i = pl.program_id(0)
    f = pl.program_id(1)
    rc = FFN_ROW_CHUNK
    n_chunks = tm // rc
    row0 = i * tm

    def tile_rows(r):
        return pl.ds(pl.multiple_of(r * rc, rc), rc)

    def hbm_rows(r):
        return pl.ds(pl.multiple_of(row0 + r * rc, rc), rc)

    def x_copy(r, slot):
        return pltpu.make_async_copy(x_hbm.at[hbm_rows(r), :], xbuf.at[slot], sem_x.at[slot])

    def out_copy(r):
        return pltpu.make_async_copy(acc_scr.at[tile_rows(r), :], out_hbm.at[hbm_rows(r), :], sem_o.at[0])

    def hmix_copy(r):
        return pltpu.make_async_copy(hn_scr.at[tile_rows(r), :], hmix_hbm.at[hbm_rows(r), :], sem_h.at[0])

    def stream_x(process):
        x_copy(0, 0).start()

        def pair(j, carry):
            r = 2 * j
            x_copy(r + 1, 1).start()
            x_copy(r, 0).wait()
            process(r, 0)

            @pl.when(r + 2 < n_chunks)
            def _():
                x_copy(r + 2, 0).start()
            x_copy(r + 1, 1).wait()
            process(r + 1, 1)
            return carry
        lax.fori_loop(0, n_chunks // 2, pair, 0)

    d = acc_scr.shape[1]
    cw = FFN_COL_BLOCK
    col_blocks = [slice(c0, c0 + cw) for c0 in range(0, d, cw)]

    def add_sumsq(part, v):
        sq = v * v
        for t in range(cw // LANES):
            part = part + sq[:, t * LANES:(t + 1) * LANES]
        return part

    def rstd_of(part):
        return lax.rsqrt(jnp.sum(part, axis=-1, keepdims=True) / d + EPS)

    def norm_mod(v, rstd, gain_ref, shift_row, scale_row, cols):
        y = v * rstd * gain_ref[:, cols]
        return y * (1.0 + mod_ref[scale_row:scale_row + 1, cols]) + mod_ref[shift_row:shift_row + 1, cols]

    @pl.when(f == 0)
    def _():
        def prologue(r, slot):
            rows = tile_rows(r)
            part = jnp.zeros((rc, LANES), F32)
            for cols in col_blocks:
                part = add_sumsq(part, xbuf[slot, :, cols])
            rstd = rstd_of(part)
            for cols in col_blocks:
                hn = norm_mod(xbuf[slot, :, cols], rstd, g_ref, 0, 1, cols)
                hn_scr[rows, cols] = hn.astype(BF16)
                acc_scr[rows, cols] = jnp.zeros((rc, cw), F32)
        stream_x(prologue)

    hn = hn_scr[...]
    gt = _bdot(hn, wg_ref)
    up = _bdot(hn, wu_ref)
    act = (_silu(gt) * up).astype(BF16)
    acc_scr[...] += _bdot(act, wd_ref)

    @pl.when(f == n_f - 1)
    def _():
        def epilogue(r, slot):
            rows = tile_rows(r)
            part = jnp.zeros((rc, LANES), F32)
            for cols in col_blocks:
                y = xbuf[slot, :, cols] + (0.5 * mod_ref[2:3, cols]) * acc_scr[rows, cols]
                acc_scr[rows, cols] = y
                part = add_sumsq(part, y)
            rstd = rstd_of(part)
            for cols in col_blocks:
                y = acc_scr[rows, cols]
                if with_mix:
                    hn_scr[rows, cols] = norm_mod(y, rstd, gmix_ref, 3, 4, cols).astype(BF16)
                if final_norm:
                    acc_scr[rows, cols] = y * rstd * gfin_ref[:, cols]
            if with_mix:
                hmix_copy(r).start()
            if write_out:
                out_copy(r).start()
        stream_x(epilogue)
        for _ in range(n_chunks):
            if with_mix:
                hmix_copy(0).wait()
            if write_out:
                out_copy(0).wait()


def _ffn(x, mod, g, wg, wu, wd, *, g_mix=None, g_final=None, write_out=True):
    m, d = x.shape
    tm = min(FFN_ROWS, m)
    tf = FFN_COLS
    dff = wg.shape[1]
    assert m % tm == 0 and dff % tf == 0 and tm % (2 * FFN_ROW_CHUNK) == 0
    n_f = dff // tf
    with_mix = g_mix is not None
    final_norm = g_final is not None
    assert write_out or with_mix
    const = lambda i, f: (0, 0)
    hbm = pl.BlockSpec(memory_space=pl.ANY)
    in_specs = [
        hbm,
        pl.BlockSpec(mod.shape, const),
        pl.BlockSpec((1, d), const),
        pl.BlockSpec((d, tf), lambda i, f: (0, f)),
        pl.BlockSpec((d, tf), lambda i, f: (0, f)),
        pl.BlockSpec((tf, d), lambda i, f: (f, 0)),
    ]
    args = [x, mod, g.reshape(1, d), wg, wu, wd]
    if with_mix:
        in_specs.append(pl.BlockSpec((1, d), const))
        args.append(g_mix.reshape(1, d))
    if final_norm:
        in_specs.append(pl.BlockSpec((1, d), const))
        args.append(g_final.reshape(1, d))
    out_shape, out_specs = [], []
    if write_out:
        out_shape.append(jax.ShapeDtypeStruct((m, d), F32))
        out_specs.append(hbm)
    if with_mix:
        out_shape.append(jax.ShapeDtypeStruct((m, d), BF16))
        out_specs.append(hbm)
    res = pl.pallas_call(
        functools.partial(_ffn_kernel, tm=tm, n_f=n_f, write_out=write_out, with_mix=with_mix,
                          final_norm=final_norm),
        grid=(m // tm, n_f),
        in_specs=in_specs,
        out_specs=out_specs,
        out_shape=out_shape,
        scratch_shapes=[
            pltpu.VMEM((tm, d), BF16),
            pltpu.VMEM((tm, d), F32),
            pltpu.VMEM((2, FFN_ROW_CHUNK, d), F32),
            pltpu.SemaphoreType.DMA((2,)),
            pltpu.SemaphoreType.DMA((1,)),
            pltpu.SemaphoreType.DMA((1,)),
        ],
        compiler_params=_params(("arbitrary", "arbitrary")),
        name="ffn_mix" if with_mix else "ffn_final",
    )(*args)
    return res if len(res) > 1 else res[0]


def _glu_kernel(a_ref, wu_ref, wg_ref, o_ref):
    a = a_ref[...]
    u = _bdot(a, wu_ref)
    g = _bdot(a, wg_ref)
    o_ref[...] = u * jax.nn.sigmoid(g)


def _glu_proj(a, w):
    m, d = a.shape
    tm = min(PROJ_ROWS, m)
    tn = GLU_COLS
    nb = D_CONV // tn
    return pl.pallas_call(
        _glu_kernel,
        grid=(m // tm, nb),
        in_specs=[
            pl.BlockSpec((tm, d), lambda i, j: (i, 0)),
            pl.BlockSpec((d, tn), lambda i, j: (0, j)),
            pl.BlockSpec((d, tn), lambda i, j: (0, j + nb)),
        ],
        out_specs=pl.BlockSpec((tm, tn), lambda i, j: (i, j)),
        out_shape=jax.ShapeDtypeStruct((m, D_CONV), F32),
        compiler_params=_params(("arbitrary", "arbitrary")),
        name="glu_proj",
    )(a, w, w)


def _qkv_kernel(a_ref, w_ref, gain_ref, cos_ref, sin_ref, o_ref, *, n_norm_tiles):
    j = pl.program_id(1)
    p = _bdot(a_ref[...], w_ref)
    tn = p.shape[1]

    @pl.when(j < n_norm_tiles)
    def _():
        cosf = cos_ref[...]
        sinf = sin_ref[...]
        even = (lax.broadcasted_iota(jnp.int32, cosf.shape, 1) % 2) == 0
        for h in range(tn // HEAD_DIM):
            cols = slice(h * HEAD_DIM, (h + 1) * HEAD_DIM)
            ph = p[:, cols]
            ms = jnp.mean(ph * ph, axis=-1, keepdims=True)
            y = ph * lax.rsqrt(ms + EPS) * gain_ref[:, cols]
            partner = jnp.where(even, pltpu.roll(y, HEAD_DIM - 1, 1), pltpu.roll(y, 1, 1))
            o_ref[:, cols] = (y * cosf + partner * sinf).astype(o_ref.dtype)

    @pl.when(j >= n_norm_tiles)
    def _():
        o_ref[...] = p.astype(o_ref.dtype)


def _qkv_proj(a, w, gain, cosf, sinf, *, col0, n_tiles, n_norm_tiles):
    m, d = a.shape
    tm = min(PROJ_ROWS, m)
    tn = QKV_COLS
    return pl.pallas_call(
        functools.partial(_qkv_kernel, n_norm_tiles=n_norm_tiles),
        grid=(m // tm, n_tiles),
        in_specs=[
            pl.BlockSpec((tm, d), lambda i, j: (i, 0)),
            pl.BlockSpec((d, tn), lambda i, j: (0, j + col0)),
            pl.BlockSpec((1, tn), lambda i, j: (0, j)),
            pl.BlockSpec((tm, HEAD_DIM), lambda i, j: (i, 0)),
            pl.BlockSpec((tm, HEAD_DIM), lambda i, j: (i, 0)),
        ],
        out_specs=pl.BlockSpec((tm, tn), lambda i, j: (i, j)),
        out_shape=jax.ShapeDtypeStruct((m, n_tiles * tn), BF16),
        compiler_params=_params(("arbitrary", "arbitrary")),
        name="qkv_proj",
    )(a, w, gain, cosf, sinf)


def _conv_kernel(prev_ref, main_ref, next_ref, w_ref, b_ref, lng_ref, lnb_ref, o_ref, e_scr, ph_scr, y_scr):
    i = pl.program_id(0)
    n = pl.num_programs(0)
    ts, ch = main_ref.shape
    e_scr[0:CONV_HALO, :] = jnp.where(i > 0, prev_ref[...], 0.0)
    e_scr[CONV_HALO:CONV_HALO + ts, :] = main_ref[...]
    e_scr[CONV_HALO + ts:CONV_HALO + ts + CONV_HALO, :] = jnp.where(i < n - 1, next_ref[...], 0.0)
    first = CONV_HALO - CONV_WIDTH // 2
    ph_rows = ph_scr.shape[1]

    def lane_tile(c, carry):
        cols = pl.ds(pl.multiple_of(c * LANES, LANES), LANES)
        for ph in range(SUBLANES):
            ph_scr[ph] = e_scr[pl.ds(ph, ph_rows), cols]
        taps = [w_ref[k:k + 1, cols] for k in range(CONV_WIDTH)]
        bias = b_ref[:, cols]
        for r in range(ts // CONV_ROW_BLOCK):
            r0 = r * CONV_ROW_BLOCK
            acc = jnp.zeros((CONV_ROW_BLOCK, LANES), F32)
            for k in range(CONV_WIDTH):
                off = first + k
                acc = acc + ph_scr[off % SUBLANES, pl.ds(r0 + off - off % SUBLANES, CONV_ROW_BLOCK), :] * taps[k]
            y_scr[pl.ds(r0, CONV_ROW_BLOCK), cols] = acc + bias
        return carry
    lax.fori_loop(0, ch // LANES, lane_tile, 0)

    lb = min(LN_ROW_BLOCK, ts)

    def ln_rows(r, carry):
        rows = pl.ds(pl.multiple_of(r * lb, lb), lb)
        y = y_scr[rows, :]
        mu = jnp.mean(y, axis=-1, keepdims=True)
        yc = y - mu
        var = jnp.mean(yc * yc, axis=-1, keepdims=True)
        z = yc * lax.rsqrt(var + EPS) * lng_ref[...] + lnb_ref[...]
        o_ref[rows, :] = _silu(z).astype(o_ref.dtype)
        return carry
    lax.fori_loop(0, ts // lb, ln_rows, 0)


def _conv_module(glu, w, b, ln_g, ln_b):
    l, ch = glu.shape
    ts = min(CONV_ROWS, l)
    hb = ts // CONV_HALO
    n = l // ts
    last_halo_block = l // CONV_HALO - 1
    ph_rows = ts + 2 * CONV_HALO - SUBLANES
    assert CONV_HALO - CONV_WIDTH // 2 + CONV_WIDTH - 1 + ts <= ph_rows + SUBLANES - 1
    return pl.pallas_call(
        _conv_kernel,
        grid=(n,),
        in_specs=[
            pl.BlockSpec((CONV_HALO, ch), lambda i: (jnp.maximum(i * hb - 1, 0), 0)),
            pl.BlockSpec((ts, ch), lambda i: (i, 0)),
            pl.BlockSpec((CONV_HALO, ch), lambda i: (jnp.minimum((i + 1) * hb, last_halo_block), 0)),
            pl.BlockSpec((CONV_WIDTH, ch), lambda i: (0, 0)),
            pl.BlockSpec((1, ch), lambda i: (0, 0)),
            pl.BlockSpec((1, ch), lambda i: (0, 0)),
            pl.BlockSpec((1, ch), lambda i: (0, 0)),
        ],
        out_specs=pl.BlockSpec((ts, ch), lambda i: (i, 0)),
        out_shape=jax.ShapeDtypeStruct((l, ch), BF16),
        scratch_shapes=[
            pltpu.VMEM((ts + 2 * CONV_HALO, ch), F32),
            pltpu.VMEM((SUBLANES, ph_rows, LANES), F32),
            pltpu.VMEM((ts, ch), F32),
        ],
        compiler_params=_params(("arbitrary",)),
        name="conv_module",
    )(glu, glu, glu, w, b.reshape(1, ch), ln_g.reshape(1, ch), ln_b.reshape(1, ch))


def _attn_kernel(q_ref, k_ref, vt_ref, o_ref, qs_scr, s0_scr, s1_scr, m_scr, l_scr, acc_scr, *, tk, n_chunks):
    tq = q_ref.shape[0]
    part = GROUP * tq // ATTN_ROW_SPLITS
    for g in range(GROUP):
        qs_scr[g * tq:(g + 1) * tq, :] = q_ref[:, g * HEAD_DIM:(g + 1) * HEAD_DIM]
    m_scr[...] = jnp.full(m_scr.shape, -jnp.inf, F32)
    l_scr[...] = jnp.zeros(l_scr.shape, F32)
    acc_scr[...] = jnp.zeros(acc_scr.shape, F32)

    def keys(c):
        return pl.ds(pl.multiple_of(c * tk, tk), tk)

    def scores(c, s_scr):
        k = k_ref[keys(c), :]
        for h in range(ATTN_ROW_SPLITS):
            r = slice(h * part, (h + 1) * part)
            s_scr[:, r] = lax.dot_general(k, qs_scr[r, :], (((1,), (1,)), ((), ())),
                                          preferred_element_type=F32)

    def accumulate(c, s_scr):
        vt = vt_ref[:, keys(c)]
        for h in range(ATTN_ROW_SPLITS):
            r = slice(h * part, (h + 1) * part)
            s = s_scr[:, r]
            m_prev = m_scr[:, r]
            m_new = jnp.maximum(m_prev, jnp.max(s, axis=0, keepdims=True))
            alpha = jnp.exp2(m_prev - m_new)
            p = jnp.exp2(s - m_new)
            l_scr[:, r] = alpha * l_scr[:, r] + jnp.sum(p, axis=0, keepdims=True)
            acc_scr[:, r] = alpha * acc_scr[:, r] + jnp.dot(vt, p.astype(BF16), preferred_element_type=F32)
            m_scr[:, r] = m_new

    scores(0, s0_scr)

    def pair(j, carry):
        scores(2 * j + 1, s1_scr)
        accumulate(2 * j, s0_scr)
        scores(2 * j + 2, s0_scr)
        accumulate(2 * j + 1, s1_scr)
        return carry
    lax.fori_loop(0, (n_chunks - 1) // 2, pair, 0)
    if n_chunks % 2 == 0:
        scores(n_chunks - 1, s1_scr)
        accumulate(n_chunks - 2, s0_scr)
        accumulate(n_chunks - 1, s1_scr)
    else:
        accumulate(n_chunks - 1, s0_scr)

    o_t = acc_scr[...] / l_scr[...]
    for g in range(GROUP):
        o_ref[:, g * HEAD_DIM:(g + 1) * HEAD_DIM] = o_t[:, g * tq:(g + 1) * tq].T.astype(o_ref.dtype)


def _key_chunk(n_keys):
    fits = [t for t in range(LANES, ATTN_MAX_KEY_CHUNK + 1, LANES) if n_keys % t == 0]
    assert fits, n_keys
    return fits[-1]


def _attention(q, k, v_t):
    l = q.shape[0]
    n_keys = k.shape[0]
    tq = min(ATTN_Q_ROWS, l)
    tk = _key_chunk(n_keys)
    gw = GROUP * HEAD_DIM
    m = GROUP * tq
    return pl.pallas_call(
        functools.partial(_attn_kernel, tk=tk, n_chunks=n_keys // tk),
        grid=(N_KV_HEADS, l // tq),
        in_specs=[
            pl.BlockSpec((tq, gw), lambda h, i: (i, h)),
            pl.BlockSpec((n_keys, HEAD_DIM), lambda h, i: (0, h)),
            pl.BlockSpec((HEAD_DIM, n_keys), lambda h, i: (h, 0)),
        ],
        out_specs=pl.BlockSpec((tq, gw), lambda h, i: (i, h)),
        out_shape=jax.ShapeDtypeStruct((l, N_Q_HEADS * HEAD_DIM), BF16),
        scratch_shapes=[
            pltpu.VMEM((m, HEAD_DIM), BF16),
            pltpu.VMEM((tk, m), F32),
            pltpu.VMEM((tk, m), F32),
            pltpu.VMEM((1, m), F32),
            pltpu.VMEM((1, m), F32),
            pltpu.VMEM((HEAD_DIM, m), F32),
        ],
        compiler_params=_params(("arbitrary", "arbitrary")),
        name="gqa_attention",
    )(q, k, v_t)


def _outproj_kernel(conv_ref, attn_ref, wc_ref, wa_ref, x_ref, gate_ref, o_ref):
    mix = _bdot(conv_ref[...], wc_ref) + _bdot(attn_ref[...], wa_ref)
    o_ref[...] = x_ref[...] + gate_ref[...] * mix


def _out_proj(conv_x, attn_x, w, x, gate):
    m, d = x.shape
    tm = min(PROJ_ROWS, m)
    tn = OUT_COLS
    kc = conv_x.shape[1]
    ka = attn_x.shape[1]
    assert kc == ka
    return pl.pallas_call(
        _outproj_kernel,
        grid=(m // tm, d // tn),
        in_specs=[
            pl.BlockSpec((tm, kc), lambda i, j: (i, 0)),
            pl.BlockSpec((tm, ka), lambda i, j: (i, 0)),
            pl.BlockSpec((kc, tn), lambda i, j: (0, j)),
            pl.BlockSpec((ka, tn), lambda i, j: (1, j)),
            pl.BlockSpec((tm, tn), lambda i, j: (i, j)),
            pl.BlockSpec((1, tn), lambda i, j: (0, j)),
        ],
        out_specs=pl.BlockSpec((tm, tn), lambda i, j: (i, j)),
        out_shape=jax.ShapeDtypeStruct((m, d), F32),
        compiler_params=_params(("arbitrary", "arbitrary")),
        name="out_proj",
    )(conv_x, attn_x, w, w, x, gate)


def _rope_tables(l):
    pos = jnp.arange(l, dtype=jnp.int32)
    row = (pos // GRID_W).astype(F32)
    col = (pos % GRID_W).astype(F32)
    axis_dim = HEAD_DIM // 2
    inv = ROPE_THETA ** (-jnp.arange(0, axis_dim, 2, dtype=F32) / axis_dim)
    ang = jnp.concatenate([row[:, None] * inv, col[:, None] * inv], axis=-1)
    cos, sin = jnp.cos(ang), jnp.sin(ang)
    cosf = jnp.repeat(cos, 2, axis=-1)
    sinf = jnp.stack([-sin, sin], axis=-1).reshape(l, HEAD_DIM)
    return cosf, sinf


def kernel(x, c, ctx, c_ctx, w_ada, b_ada, g_ffn1, w1_gate, w1_up, w1_down, g_mix, w_in, conv_w, conv_b,
           conv_ln_g, conv_ln_b, q_norm_g, k_norm_g, w_out, g_ffn2, w2_gate, w2_up, w2_down, g_final):
    b, l, d = x.shape
    depth = w_ada.shape[0]
    assert b == 1 and depth == 1 and l % GRID_W == 0
    n_ctx = ctx.shape[1]
    xs = x[0]
    cs = ctx[0]

    cc = jnp.zeros((SUBLANES, d), F32).at[0].set(c[0]).at[1].set(c_ctx)
    mods = _ada(cc, w_ada[0], b_ada[0])
    mx = mods[0].reshape(-1, d)
    mc = mods[1].reshape(-1, d)

    x1, hx = _ffn(xs, mx[0:5], g_ffn1[0], w1_gate[0], w1_up[0], w1_down[0], g_mix=g_mix[0])
    hc = _ffn(cs, mc[0:5], g_ffn1[0], w1_gate[0], w1_up[0], w1_down[0], g_mix=g_mix[0], write_out=False)

    win = w_in[0]
    glu = _glu_proj(hx, win)
    q_gain = jnp.tile(q_norm_g[0] * (ATTN_SCALE * LOG2E), N_Q_HEADS)
    k_gain = jnp.tile(k_norm_g[0], N_KV_HEADS)
    v_pad = jnp.ones((N_KV_HEADS * HEAD_DIM,), F32)
    kv_gain = jnp.concatenate([k_gain, v_pad]).reshape(1, -1)
    cosf, sinf = _rope_tables(l)
    q_tiles = N_Q_HEADS * HEAD_DIM // QKV_COLS
    k_tiles = N_KV_HEADS * HEAD_DIM // QKV_COLS
    col_q = 2 * D_CONV // QKV_COLS
    q = _qkv_proj(hx, win, q_gain.reshape(1, -1), cosf, sinf, col0=col_q, n_tiles=q_tiles, n_norm_tiles=q_tiles)
    kv_x = _qkv_proj(hx, win, kv_gain, cosf, sinf, col0=col_q + q_tiles, n_tiles=2 * k_tiles,
                     n_norm_tiles=k_tiles)
    kv_c = _qkv_proj(hc, win, kv_gain, jnp.ones((n_ctx, HEAD_DIM), F32), jnp.zeros((n_ctx, HEAD_DIM), F32),
                     col0=col_q + q_tiles, n_tiles=2 * k_tiles, n_norm_tiles=k_tiles)
    kv = jnp.concatenate([kv_x, kv_c], axis=0)
    n_k = N_KV_HEADS * HEAD_DIM
    keys, values_t = kv[:, :n_k], kv[:, n_k:].T

    conv_x = _conv_module(glu, conv_w[0], conv_b[0], conv_ln_g[0], conv_ln_b[0])
    attn_x = _attention(q, keys, values_t)

    x2 = _out_proj(conv_x, attn_x, w_out[0], x1, mx[5:6])

    out = _ffn(x2, mx[6:9], g_ffn2[0], w2_gate[0], w2_up[0], w2_down[0], g_final=g_final)
    return out[None]
```

```python
import functools
import math

import jax
import jax.numpy as jnp
from jax import lax
from jax.experimental import pallas as pl
from jax.experimental.pallas import tpu as pltpu

F32 = jnp.float32
BF16 = jnp.bfloat16

EPS = 1e-6
HEAD_DIM = 128
N_Q_HEADS = 16
N_KV_HEADS = 4
GROUP = N_Q_HEADS // N_KV_HEADS
D_CONV = 2048
CONV_WIDTH = 31
GRID_W = 64
ROPE_THETA = 10000.0
LOG2E = math.log2(math.e)
ATTN_SCALE = 1.0 / math.sqrt(HEAD_DIM)

V7X_VMEM_BYTES = 64 * 1024 * 1024
VMEM_LIMIT_BYTES = V7X_VMEM_BYTES - 4 * 1024 * 1024
LANES = 128
SUBLANES = 8
MXU_WIDTH = 256

FFN_ROWS = 1024
FFN_COLS = MXU_WIDTH
FFN_ROW_CHUNK = 64
FFN_COL_BLOCK = 512
ADA_COLS = 512
PROJ_ROWS = 1024
PROJ_ROW_SPLITS = 4
GLU_COLS = 256
QKV_COLS = 512
OUT_COLS = 512
CONV_ROWS = 256
CONV_HALO = 16
CONV_ROW_BLOCK = 32
LN_ROW_BLOCK = 128
ATTN_Q_ROWS = 256
ATTN_ROW_SPLITS = 2
ATTN_MAX_KEY_CHUNK = 1408


def _params(semantics):
    return pltpu.CompilerParams(dimension_semantics=semantics, vmem_limit_bytes=VMEM_LIMIT_BYTES)


def _silu(v):
    return v * jax.nn.sigmoid(v)


def _bdot(a, w_ref):
    return jnp.dot(a, w_ref[...].astype(BF16), preferred_element_type=F32)


def _ada_kernel(c_ref, w_ref, b_ref, o_ref):
    a = _silu(c_ref[...]).astype(BF16)
    o_ref[...] = _bdot(a, w_ref) + b_ref[...]


def _ada(cc, w, b, n_cols):
    rows, d = cc.shape
    tn = ADA_COLS
    assert n_cols % tn == 0
    return pl.pallas_call(
        _ada_kernel,
        grid=(n_cols // tn,),
        in_specs=[
            pl.BlockSpec((rows, d), lambda j: (0, 0)),
            pl.BlockSpec((d, tn), lambda j: (0, j)),
            pl.BlockSpec((1, tn), lambda j: (0, j)),
        ],
        out_specs=pl.BlockSpec((rows, tn), lambda j: (0, j)),
        out_shape=jax.ShapeDtypeStruct((rows, n_cols), F32),
        compiler_params=_params(("arbitrary",)),
        name="ada_mod",
    )(cc, w, b)


def _ffn_kernel(*refs, tm, n_f, write_out, with_mix, final_norm):
    refs = list(refs)
    x_hbm, mod_ref, g_ref, wg_ref, wu_ref, wd_ref = refs[:6]
    del refs[:6]
    gmix_ref = refs.pop(0) if with_mix else None
    gfin_ref = refs.pop(0) if final_norm else None
    out_hbm = refs.pop(0) if write_out else None
    hmix_hbm = refs.pop(0) if with_mix else None
    hn_scr, acc_scr, xbuf, sem_x, sem_o, sem_h = refs

    i = pl.program_id(0)
    f = pl.program_id(1)
    rc = FFN_ROW_CHUNK
    n_chunks = tm // rc
    row0 = i * tm

    def tile_rows(r):
        return pl.ds(pl.multiple_of(r * rc, rc), rc)

    def hbm_rows(r):
        return pl.ds(pl.multiple_of(row0 + r * rc, rc), rc)

    def x_copy(r, slot):
        return pltpu.make_async_copy(x_hbm.at[hbm_rows(r), :], xbuf.at[slot], sem_x.at[slot])

    def out_copy(r):
        return pltpu.make_async_copy(acc_scr.at[tile_rows(r), :], out_hbm.at[hbm_rows(r), :], sem_o.at[0])

    def hmix_copy(r):
        return pltpu.make_async_copy(hn_scr.at[tile_rows(r), :], hmix_hbm.at[hbm_rows(r), :], sem_h.at[0])

    def stream_x(process):
        x_copy(0, 0).start()

        def pair(j, carry):
            r = 2 * j
            x_copy(r + 1, 1).start()
            x_copy(r, 0).wait()
            process(r, 0)

            @pl.when(r + 2 < n_chunks)
            def _():
                x_copy(r + 2, 0).start()
            x_copy(r + 1, 1).wait()
            process(r + 1, 1)
            return carry
        lax.fori_loop(0, n_chunks // 2, pair, 0)

    d = acc_scr.shape[1]
    cw = FFN_COL_BLOCK
    col_blocks = [slice(c0, c0 + cw) for c0 in range(0, d, cw)]

    def add_sumsq(part, v):
        sq = v * v
        for t in range(cw // LANES):
            part = part + sq[:, t * LANES:(t + 1) * LANES]
        return part

    def rstd_of(part):
        return lax.rsqrt(jnp.sum(part, axis=-1, keepdims=True) / d + EPS)

    def norm_mod(v, rstd, gain_ref, shift_row, scale_row, cols):
        y = v * rstd * gain_ref[:, cols]
        return y * (1.0 + mod_ref[scale_row:scale_row + 1, cols]) + mod_ref[shift_row:shift_row + 1, cols]

    @pl.when(f == 0)
    def _():
        def prologue(r, slot):
            rows = tile_rows(r)
            part = jnp.zeros((rc, LANES), F32)
            for cols in col_blocks:
                part = add_sumsq(part, xbuf[slot, :, cols])
            rstd = rstd_of(part)
            for cols in col_blocks:
                hn = norm_mod(xbuf[slot, :, cols], rstd, g_ref, 0, 1, cols)
                hn_scr[rows, cols] = hn.astype(BF16)
                acc_scr[rows, cols] = jnp.zeros((rc, cw), F32)
        stream_x(prologue)

    hn = hn_scr[...]
    gt = _bdot(hn, wg_ref)
    up = _bdot(hn, wu_ref)
    act = (_silu(gt) * up).astype(BF16)
    acc_scr[...] += _bdot(act, wd_ref)

    @pl.when(f == n_f - 1)
    def _():
        def epilogue(r, slot):
            rows = tile_rows(r)
            part = jnp.zeros((rc, LANES), F32)
            for cols in col_blocks:
                y = xbuf[slot, :, cols] + (0.5 * mod_ref[2:3, cols]) * acc_scr[rows, cols]
                acc_scr[rows, cols] = y
                part = add_sumsq(part, y)
            rstd = rstd_of(part)
            for cols in col_blocks:
                y = acc_scr[rows, cols]
                if with_mix:
                    hn_scr[rows, cols] = norm_mod(y, rstd, gmix_ref, 3, 4, cols).astype(BF16)
                if final_norm:
                    acc_scr[rows, cols] = y * rstd * gfin_ref[:, cols]
            if with_mix:
                hmix_copy(r).start()
            if write_out:
                out_copy(r).start()
        stream_x(epilogue)
        for _ in range(n_chunks):
            if with_mix:
                hmix_copy(0).wait()
            if write_out:
                out_copy(0).wait()


def _ffn(x, mod, g, wg, wu, wd, *, g_mix=None, g_final=None, write_out=True):
    m, d = x.shape
    tm = min(FFN_ROWS, m)
    tf = FFN_COLS
    dff = wg.shape[1]
    assert m % tm == 0 and dff % tf == 0 and tm % (2 * FFN_ROW_CHUNK) == 0
    n_f = dff // tf
    with_mix = g_mix is not None
    final_norm = g_final is not None
    assert write_out or with_mix
    const = lambda i, f: (0, 0)
    hbm = pl.BlockSpec(memory_space=pl.ANY)
    in_specs = [
        hbm,
        pl.BlockSpec(mod.shape, const),
        pl.BlockSpec((1, d), const),
        pl.BlockSpec((d, tf), lambda i, f: (0, f)),
        pl.BlockSpec((d, tf), lambda i, f: (0, f)),
        pl.BlockSpec((tf, d), lambda i, f: (f, 0)),
    ]
    args = [x, mod, g.reshape(1, d), wg, wu, wd]
    if with_mix:
        in_specs.append(pl.BlockSpec((1, d), const))
        args.append(g_mix.reshape(1, d))
    if final_norm:
        in_specs.append(pl.BlockSpec((1, d), const))
        args.append(g_final.reshape(1, d))
    out_shape, out_specs = [], []
    if write_out:
        out_shape.append(jax.ShapeDtypeStruct((m, d), F32))
        out_specs.append(hbm)
    if with_mix:
        out_shape.append(jax.ShapeDtypeStruct((m, d), BF16))
        out_specs.append(hbm)
    res = pl.pallas_call(
        functools.partial(_ffn_kernel, tm=tm, n_f=n_f, write_out=write_out, with_mix=with_mix,
                          final_norm=final_norm),
        grid=(m // tm, n_f),
        in_specs=in_specs,
        out_specs=out_specs,
        out_shape=out_shape,
        scratch_shapes=[
            pltpu.VMEM((tm, d), BF16),
            pltpu.VMEM((tm, d), F32),
            pltpu.VMEM((2, FFN_ROW_CHUNK, d), F32),
            pltpu.SemaphoreType.DMA((2,)),
            pltpu.SemaphoreType.DMA((1,)),
            pltpu.SemaphoreType.DMA((1,)),
        ],
        compiler_params=_params(("arbitrary", "arbitrary")),
        name="ffn_mix" if with_mix else "ffn_final",
    )(*args)
    return res if len(res) > 1 else res[0]


def _glu_kernel(a_ref, wu_ref, wg_ref, o_ref):
    a = a_ref[...]
    u = _bdot(a, wu_ref)
    g = _bdot(a, wg_ref)
    o_ref[...] = u * jax.nn.sigmoid(g)


def _glu_proj(a, w):
    m, d = a.shape
    tm = min(PROJ_ROWS, m)
    tn = GLU_COLS
    nb = D_CONV // tn
    return pl.pallas_call(
        _glu_kernel,
        grid=(m // tm, nb),
        in_specs=[
            pl.BlockSpec((tm, d), lambda i, j: (i, 0)),
            pl.BlockSpec((d, tn), lambda i, j: (0, j)),
            pl.BlockSpec((d, tn), lambda i, j: (0, j + nb)),
        ],
        out_specs=pl.BlockSpec((tm, tn), lambda i, j: (i, j)),
        out_shape=jax.ShapeDtypeStruct((m, D_CONV), F32),
        compiler_params=_params(("arbitrary", "arbitrary")),
        name="glu_proj",
    )(a, w, w)


def _project_norm_rope(a_ref, w_ref, gain_ref, cos_ref, sin_ref, o_ref):
    tm = a_ref.shape[0]
    part = tm // PROJ_ROW_SPLITS
    for s in range(PROJ_ROW_SPLITS):
        rows = slice(s * part, (s + 1) * part)
        p = _bdot(a_ref[rows, :], w_ref)
        cosf = cos_ref[rows, :]
        sinf = sin_ref[rows, :]
        even = (lax.broadcasted_iota(jnp.int32, cosf.shape, 1) % 2) == 0
        for h in range(p.shape[1] // HEAD_DIM):
            cols = slice(h * HEAD_DIM, (h + 1) * HEAD_DIM)
            ph = p[:, cols]
            ms = jnp.mean(ph * ph, axis=-1, keepdims=True)
            y = ph * lax.rsqrt(ms + EPS) * gain_ref[:, cols]
            partner = jnp.where(even, pltpu.roll(y, HEAD_DIM - 1, 1), pltpu.roll(y, 1, 1))
            o_ref[rows, cols] = (y * cosf + partner * sinf).astype(o_ref.dtype)


def _q_kernel(a_ref, w_ref, gain_ref, cos_ref, sin_ref, o_ref):
    _project_norm_rope(a_ref, w_ref, gain_ref, cos_ref, sin_ref, o_ref)


def _kv_kernel(a_ref, w_ref, gain_ref, cos_ref, sin_ref, k_ref, vt_ref):
    j = pl.program_id(1)

    @pl.when(j == 0)
    def _():
        _project_norm_rope(a_ref, w_ref, gain_ref, cos_ref, sin_ref, k_ref)

    @pl.when(j == 1)
    def _():
        tm = a_ref.shape[0]
        part = tm // PROJ_ROW_SPLITS
        for s in range(PROJ_ROW_SPLITS):
            rows = slice(s * part, (s + 1) * part)
            vt_ref[:, rows] = _bdot(a_ref[rows, :], w_ref).T.astype(vt_ref.dtype)


def _q_proj(a, w, gain, cosf, sinf, *, col0):
    m, d = a.shape
    tm = min(PROJ_ROWS, m)
    tn = QKV_COLS
    n = N_Q_HEADS * HEAD_DIM
    return pl.pallas_call(
        _q_kernel,
        grid=(m // tm, n // tn),
        in_specs=[
            pl.BlockSpec((tm, d), lambda i, j: (i, 0)),
            pl.BlockSpec((d, tn), lambda i, j: (0, j + col0 // tn)),
            pl.BlockSpec((1, tn), lambda i, j: (0, j)),
            pl.BlockSpec((tm, HEAD_DIM), lambda i, j: (i, 0)),
            pl.BlockSpec((tm, HEAD_DIM), lambda i, j: (i, 0)),
        ],
        out_specs=pl.BlockSpec((tm, tn), lambda i, j: (i, j)),
        out_shape=jax.ShapeDtypeStruct((m, n), BF16),
        compiler_params=_params(("arbitrary", "arbitrary")),
        name="q_proj",
    )(a, w, gain, cosf, sinf)


def _kv_proj(a, w, gain, cosf, sinf, *, col0):
    m, d = a.shape
    tm = min(PROJ_ROWS, m)
    n = N_KV_HEADS * HEAD_DIM
    assert n == QKV_COLS
    return pl.pallas_call(
        _kv_kernel,
        grid=(m // tm, 2),
        in_specs=[
            pl.BlockSpec((tm, d), lambda i, j: (i, 0)),
            pl.BlockSpec((d, n), lambda i, j: (0, j + col0 // n)),
            pl.BlockSpec((1, n), lambda i, j: (0, 0)),
            pl.BlockSpec((tm, HEAD_DIM), lambda i, j: (i, 0)),
            pl.BlockSpec((tm, HEAD_DIM), lambda i, j: (i, 0)),
        ],
        out_specs=[
            pl.BlockSpec((tm, n), lambda i, j: (i, 0)),
            pl.BlockSpec((n, tm), lambda i, j: (0, i)),
        ],
        out_shape=[jax.ShapeDtypeStruct((m, n), BF16), jax.ShapeDtypeStruct((n, m), BF16)],
        compiler_params=_params(("arbitrary", "arbitrary")),
        name="kv_proj",
    )(a, w, gain, cosf, sinf)


def _conv_kernel(prev_ref, main_ref, next_ref, w_ref, b_ref, lng_ref, lnb_ref, c_ref, wada_ref, bada_ref,
                 o_ref, mod_ref, e_scr, ph_scr, y_scr):
    _ada_kernel(c_ref, wada_ref, bada_ref, mod_ref)
    i = pl.program_id(0)
    n = pl.num_programs(0)
    ts, ch = main_ref.shape
    e_scr[0:CONV_HALO, :] = jnp.where(i > 0, prev_ref[...], 0.0)
    e_scr[CONV_HALO:CONV_HALO + ts, :] = main_ref[...]
    e_scr[CONV_HALO + ts:CONV_HALO + ts + CONV_HALO, :] = jnp.where(i < n - 1, next_ref[...], 0.0)
    first = CONV_HALO - CONV_WIDTH // 2
    ph_rows = ph_scr.shape[1]

    def lane_tile(c, carry):
        cols = pl.ds(pl.multiple_of(c * LANES, LANES), LANES)
        for ph in range(SUBLANES):
            ph_scr[ph] = e_scr[pl.ds(ph, ph_rows), cols]
        taps = [w_ref[k:k + 1, cols] for k in range(CONV_WIDTH)]
        bias = b_ref[:, cols]
        for r in range(ts // CONV_ROW_BLOCK):
            r0 = r * CONV_ROW_BLOCK
            acc = jnp.zeros((CONV_ROW_BLOCK, LANES), F32)
            for k in range(CONV_WIDTH):
                off = first + k
                acc = acc + ph_scr[off % SUBLANES, pl.ds(r0 + off - off % SUBLANES, CONV_ROW_BLOCK), :] * taps[k]
            y_scr[pl.ds(r0, CONV_ROW_BLOCK), cols] = acc + bias
        return carry
    lax.fori_loop(0, ch // LANES, lane_tile, 0)

    lb = min(LN_ROW_BLOCK, ts)

    def ln_rows(r, carry):
        rows = pl.ds(pl.multiple_of(r * lb, lb), lb)
        y = y_scr[rows, :]
        mu = jnp.mean(y, axis=-1, keepdims=True)
        yc = y - mu
        var = jnp.mean(yc * yc, axis=-1, keepdims=True)
        z = yc * lax.rsqrt(var + EPS) * lng_ref[...] + lnb_ref[...]
        o_ref[rows, :] = _silu(z).astype(o_ref.dtype)
        return carry
    lax.fori_loop(0, ts // lb, ln_rows, 0)


def _conv_module(glu, w, b, ln_g, ln_b, cc, w_ada, b_ada, ada_col0):
    l, ch = glu.shape
    ts = min(CONV_ROWS, l)
    hb = ts // CONV_HALO
    n = l // ts
    last_halo_block = l // CONV_HALO - 1
    mod_rows, d = cc.shape
    n_late = w_ada.shape[1] - ada_col0
    tn = n_late // n
    assert tn * n == n_late and tn % LANES == 0 and ada_col0 % tn == 0
    ada_blk0 = ada_col0 // tn
    ph_rows = ts + 2 * CONV_HALO - SUBLANES
    assert CONV_HALO - CONV_WIDTH // 2 + CONV_WIDTH - 1 + ts <= ph_rows + SUBLANES - 1
    return pl.pallas_call(
        _conv_kernel,
        grid=(n,),
        in_specs=[
            pl.BlockSpec((CONV_HALO, ch), lambda i: (jnp.maximum(i * hb - 1, 0), 0)),
            pl.BlockSpec((ts, ch), lambda i: (i, 0)),
            pl.BlockSpec((CONV_HALO, ch), lambda i: (jnp.minimum((i + 1) * hb, last_halo_block), 0)),
            pl.BlockSpec((CONV_WIDTH, ch), lambda i: (0, 0)),
            pl.BlockSpec((1, ch), lambda i: (0, 0)),
            pl.BlockSpec((1, ch), lambda i: (0, 0)),
            pl.BlockSpec((1, ch), lambda i: (0, 0)),
            pl.BlockSpec((mod_rows, d), lambda i: (0, 0)),
            pl.BlockSpec((d, tn), lambda i: (0, i + ada_blk0)),
            pl.BlockSpec((1, tn), lambda i: (0, i + ada_blk0)),
        ],
        out_specs=[
            pl.BlockSpec((ts, ch), lambda i: (i, 0)),
            pl.BlockSpec((mod_rows, tn), lambda i: (0, i)),
        ],
        out_shape=[
            jax.ShapeDtypeStruct((l, ch), BF16),
            jax.ShapeDtypeStruct((mod_rows, n_late), F32),
        ],
        scratch_shapes=[
            pltpu.VMEM((ts + 2 * CONV_HALO, ch), F32),
            pltpu.VMEM((SUBLANES, ph_rows, LANES), F32),
            pltpu.VMEM((ts, ch), F32),
        ],
        compiler_params=_params(("arbitrary",)),
        name="conv_module",
    )(glu, glu, glu, w, b.reshape(1, ch), ln_g.reshape(1, ch), ln_b.reshape(1, ch), cc, w_ada, b_ada)


def _attn_kernel(q_ref, kx_ref, vtx_ref, kc_ref, vtc_ref, o_ref, qs_scr, s0_scr, s1_scr, m_scr, l_scr, acc_scr,
                 klast_scr, vtlast_scr, *, tk, n_chunks):
    tq = q_ref.shape[0]
    part = GROUP * tq // ATTN_ROW_SPLITS
    last = n_chunks - 1

    @pl.when(pl.program_id(1) == 0)
    def _():
        n_lat = kx_ref.shape[0]
        rem = n_lat - last * tk
        if rem:
            klast_scr[0:rem, :] = kx_ref[n_lat - rem:n_lat, :]
            vtlast_scr[:, 0:rem] = vtx_ref[:, n_lat - rem:n_lat]
        klast_scr[rem:tk, :] = kc_ref[...]
        vtlast_scr[:, rem:tk] = vtc_ref[...]

    for g in range(GROUP):
        qs_scr[g * tq:(g + 1) * tq, :] = q_ref[:, g * HEAD_DIM:(g + 1) * HEAD_DIM]
    m_scr[...] = jnp.full(m_scr.shape, -jnp.inf, F32)
    l_scr[...] = jnp.zeros(l_scr.shape, F32)
    acc_scr[...] = jnp.zeros(acc_scr.shape, F32)

    def keys(c):
        return pl.ds(pl.multiple_of(c * tk, tk), tk)

    def is_last(c):
        return isinstance(c, int) and c == last

    def scores(c, s_scr):
        k = klast_scr[...] if is_last(c) else kx_ref[keys(c), :]
        for h in range(ATTN_ROW_SPLITS):
            r = slice(h * part, (h + 1) * part)
            s_scr[:, r] = lax.dot_general(k, qs_scr[r, :], (((1,), (1,)), ((), ())),
                                          preferred_element_type=F32)

    def accumulate(c, s_scr):
        vt = vtlast_scr[...] if is_last(c) else vtx_ref[:, keys(c)]
        for h in range(ATTN_ROW_SPLITS):
            r = slice(h * part, (h + 1) * part)
            s = s_scr[:, r]
            m_prev = m_scr[:, r]
            m_new = jnp.maximum(m_prev, jnp.max(s, axis=0, keepdims=True))
            alpha = jnp.exp2(m_prev - m_new)
            p = jnp.exp2(s - m_new)
            l_scr[:, r] = alpha * l_scr[:, r] + jnp.sum(p, axis=0, keepdims=True)
            acc_scr[:, r] = alpha * acc_scr[:, r] + jnp.dot(vt, p.astype(BF16), preferred_element_type=F32)
            m_scr[:, r] = m_new

    bufs = (s0_scr, s1_scr)
    scores(0, s0_scr)
    n_pairs = max((n_chunks - 2) // 2, 0)

    def pair(j, carry):
        scores(2 * j + 1, s1_scr)
        accumulate(2 * j, s0_scr)
        scores(2 * j + 2, s0_scr)
        accumulate(2 * j + 1, s1_scr)
        return carry
    lax.fori_loop(0, n_pairs, pair, 0)
    for c in range(2 * n_pairs, n_chunks):
        if c + 1 < n_chunks:
            scores(c + 1, bufs[(c + 1) % 2])
        accumulate(c, bufs[c % 2])

    o_t = acc_scr[...] / l_scr[...]
    for g in range(GROUP):
        o_ref[:, g * HEAD_DIM:(g + 1) * HEAD_DIM] = o_t[:, g * tq:(g + 1) * tq].T.astype(o_ref.dtype)


def _key_chunk(n_keys):
    fits = [t for t in range(LANES, ATTN_MAX_KEY_CHUNK + 1, LANES) if n_keys % t == 0]
    assert fits, n_keys
    return fits[-1]


def _attention(q, k_x, vt_x, k_c, vt_c):
    l = q.shape[0]
    n_ctx = k_c.shape[0]
    n_keys = l + n_ctx
    tq = min(ATTN_Q_ROWS, l)
    tk = _key_chunk(n_keys)
    n_chunks = n_keys // tk
    rem = l - (n_chunks - 1) * tk
    assert 0 <= rem and rem + n_ctx == tk and rem % LANES == 0
    gw = GROUP * HEAD_DIM
    m = GROUP * tq
    return pl.pallas_call(
        functools.partial(_attn_kernel, tk=tk, n_chunks=n_chunks),
        grid=(N_KV_HEADS, l // tq),
        in_specs=[
            pl.BlockSpec((tq, gw), lambda h, i: (i, h)),
            pl.BlockSpec((l, HEAD_DIM), lambda h, i: (0, h)),
            pl.BlockSpec((HEAD_DIM, l), lambda h, i: (h, 0)),
            pl.BlockSpec((n_ctx, HEAD_DIM), lambda h, i: (0, h)),
            pl.BlockSpec((HEAD_DIM, n_ctx), lambda h, i: (h, 0)),
        ],
        out_specs=pl.BlockSpec((tq, gw), lambda h, i: (i, h)),
        out_shape=jax.ShapeDtypeStruct((l, N_Q_HEADS * HEAD_DIM), BF16),
        scratch_shapes=[
            pltpu.VMEM((m, HEAD_DIM), BF16),
            pltpu.VMEM((tk, m), F32),
            pltpu.VMEM((tk, m), F32),
            pltpu.VMEM((1, m), F32),
            pltpu.VMEM((1, m), F32),
            pltpu.VMEM((HEAD_DIM, m), F32),
            pltpu.VMEM((tk, HEAD_DIM), BF16),
            pltpu.VMEM((HEAD_DIM, tk), BF16),
        ],
        compiler_params=_params(("arbitrary", "arbitrary")),
        name="gqa_attention",
    )(q, k_x, vt_x, k_c, vt_c)


def _outproj_kernel(conv_ref, attn_ref, wc_ref, wa_ref, x_ref, gate_ref, o_ref):
    mix = _bdot(conv_ref[...], wc_ref) + _bdot(attn_ref[...], wa_ref)
    o_ref[...] = x_ref[...] + gate_ref[...] * mix


def _out_proj(conv_x, attn_x, w, x, gate):
    m, d = x.shape
    tm = min(PROJ_ROWS, m)
    tn = OUT_COLS
    kc = conv_x.shape[1]
    ka = attn_x.shape[1]
    assert kc == ka
    return pl.pallas_call(
        _outproj_kernel,
        grid=(m // tm, d // tn),
        in_specs=[
            pl.BlockSpec((tm, kc), lambda i, j: (i, 0)),
            pl.BlockSpec((tm, ka), lambda i, j: (i, 0)),
            pl.BlockSpec((kc, tn), lambda i, j: (0, j)),
            pl.BlockSpec((ka, tn), lambda i, j: (1, j)),
            pl.BlockSpec((tm, tn), lambda i, j: (i, j)),
            pl.BlockSpec((1, tn), lambda i, j: (0, j)),
        ],
        out_specs=pl.BlockSpec((tm, tn), lambda i, j: (i, j)),
        out_shape=jax.ShapeDtypeStruct((m, d), F32),
        compiler_params=_params(("arbitrary", "arbitrary")),
        name="out_proj",
    )(conv_x, attn_x, w, w, x, gate)


def _rope_tables(l):
    pos = jnp.arange(l, dtype=jnp.int32)
    row = (pos // GRID_W).astype(F32)
    col = (pos % GRID_W).astype(F32)
    axis_dim = HEAD_DIM // 2
    inv = ROPE_THETA ** (-jnp.arange(0, axis_dim, 2, dtype=F32) / axis_dim)
    ang = jnp.concatenate([row[:, None] * inv, col[:, None] * inv], axis=-1)
    cos, sin = jnp.cos(ang), jnp.sin(ang)
    cosf = jnp.repeat(cos, 2, axis=-1)
    sinf = jnp.stack([-sin, sin], axis=-1).reshape(l, HEAD_DIM)
    return cosf, sinf


def kernel(x, c, ctx, c_ctx, w_ada, b_ada, g_ffn1, w1_gate, w1_up, w1_down, g_mix, w_in, conv_w, conv_b,
           conv_ln_g, conv_ln_b, q_norm_g, k_norm_g, w_out, g_ffn2, w2_gate, w2_up, w2_down, g_final):
    b, l, d = x.shape
    depth = w_ada.shape[0]
    assert b == 1 and depth == 1 and l % GRID_W == 0
    n_ctx = ctx.shape[1]
    xs = x[0]
    cs = ctx[0]

    cc = jnp.zeros((SUBLANES, d), F32).at[0].set(c[0]).at[1].set(c_ctx)
    n_early = 5 * d
    b_ada_row = b_ada[0].reshape(1, -1)
    mods = _ada(cc, w_ada[0], b_ada_row, n_early)
    mx = mods[0].reshape(-1, d)
    mc = mods[1].reshape(-1, d)

    x1, hx = _ffn(xs, mx[0:5], g_ffn1[0], w1_gate[0], w1_up[0], w1_down[0], g_mix=g_mix[0])
    hc = _ffn(cs, mc[0:5], g_ffn1[0], w1_gate[0], w1_up[0], w1_down[0], g_mix=g_mix[0], write_out=False)

    win = w_in[0]
    glu = _glu_proj(hx, win)
    q_gain = jnp.tile(q_norm_g[0] * (ATTN_SCALE * LOG2E), N_Q_HEADS).reshape(1, -1)
    k_gain = jnp.tile(k_norm_g[0], N_KV_HEADS).reshape(1, -1)
    cosf, sinf = _rope_tables(l)
    col_q = 2 * D_CONV
    col_k = col_q + N_Q_HEADS * HEAD_DIM
    q = _q_proj(hx, win, q_gain, cosf, sinf, col0=col_q)
    k_x, vt_x = _kv_proj(hx, win, k_gain, cosf, sinf, col0=col_k)
    k_c, vt_c = _kv_proj(hc, win, k_gain, jnp.ones((n_ctx, HEAD_DIM), F32), jnp.zeros((n_ctx, HEAD_DIM), F32),
                         col0=col_k)

    conv_x, mods_late = _conv_module(glu, conv_w[0], conv_b[0], conv_ln_g[0], conv_ln_b[0],
                                     cc, w_ada[0], b_ada_row, n_early)
    mx_late = mods_late[0].reshape(-1, d)
    attn_x = _attention(q, k_x, vt_x, k_c, vt_c)

    x2 = _out_proj(conv_x, attn_x, w_out[0], x1, mx_late[0:1])

    out = _ffn(x2, mx_late[1:4], g_ffn2[0], w2_gate[0], w2_up[0], w2_down[0], g_final=g_final)
    return out[None]
```

```python
import functools
import math

import jax
import jax.numpy as jnp
from jax import lax
from jax.experimental import pallas as pl
from jax.experimental.pallas import tpu as pltpu

F32 = jnp.float32
BF16 = jnp.bfloat16

EPS = 1e-6
HEAD_DIM = 128
N_Q_HEADS = 16
N_KV_HEADS = 4
GROUP = N_Q_HEADS // N_KV_HEADS
D_CONV = 2048
CONV_WIDTH = 31
GRID_W = 64
ROPE_THETA = 10000.0
LOG2E = math.log2(math.e)
ATTN_SCALE = 1.0 / math.sqrt(HEAD_DIM)

V7X_VMEM_BYTES = 64 * 1024 * 1024
VMEM_LIMIT_BYTES = V7X_VMEM_BYTES - 4 * 1024 * 1024
LANES = 128
SUBLANES = 8
MXU_WIDTH = 256

FFN_ROWS = 1024
FFN_COLS = MXU_WIDTH
FFN_ROW_CHUNK = 64
FFN_COL_BLOCK = 512
ADA_COLS = 512
PROJ_ROWS = 1024
PROJ_ROW_SPLITS = 4
GLU_COLS = 256
QKV_COLS = 512
OUT_COLS = 512
CONV_ROWS = 256
CONV_HALO = 16
CONV_ROW_BLOCK = 32
LN_ROW_BLOCK = 128
ATTN_Q_ROWS = 256
ATTN_ROW_SPLITS = 2
ATTN_MAX_KEY_CHUNK = 1408


def _params(semantics):
    return pltpu.CompilerParams(dimension_semantics=semantics, vmem_limit_bytes=VMEM_LIMIT_BYTES)


def _silu(v):
    return v * jax.nn.sigmoid(v)


def _bdot(a, w_ref):
    return jnp.dot(a, w_ref[...].astype(BF16), preferred_element_type=F32)


def _ada_kernel(c_ref, w_ref, b_ref, o_ref):
    a = _silu(c_ref[...]).astype(BF16)
    o_ref[...] = _bdot(a, w_ref) + b_ref[...]


def _ada(cc, w, b, n_cols):
    rows, d = cc.shape
    tn = ADA_COLS
    assert n_cols % tn == 0
    return pl.pallas_call(
        _ada_kernel,
        grid=(n_cols // tn,),
        in_specs=[
            pl.BlockSpec((rows, d), lambda j: (0, 0)),
            pl.BlockSpec((d, tn), lambda j: (0, j)),
            pl.BlockSpec((1, tn), lambda j: (0, j)),
        ],
        out_specs=pl.BlockSpec((rows, tn), lambda j: (0, j)),
        out_shape=jax.ShapeDtypeStruct((rows, n_cols), F32),
        compiler_params=_params(("arbitrary",)),
        name="ada_mod",
    )(cc, w, b)


class _RowGroup:
    def __init__(self, x_hbm, mod_ref, g_ref, gmix_ref, gfin_ref, out_hbm, hmix_hbm, scratch, *, scr0, hbm0, n_rows):
        self.x_hbm, self.mod_ref, self.g_ref, self.gmix_ref, self.gfin_ref = x_hbm, mod_ref, g_ref, gmix_ref, gfin_ref
        self.out_hbm, self.hmix_hbm = out_hbm, hmix_hbm
        self.hn_scr, self.acc_scr, self.xbuf, self.sem_x, self.sem_o, self.sem_h = scratch
        self.scr0, self.hbm0 = scr0, hbm0
        self.rc = FFN_ROW_CHUNK
        self.n_chunks = n_rows // self.rc
        assert self.n_chunks * self.rc == n_rows and self.n_chunks % 2 == 0
        self.d = self.acc_scr.shape[1]
        self.cw = FFN_COL_BLOCK
        self.col_blocks = [slice(c0, c0 + self.cw) for c0 in range(0, self.d, self.cw)]

    def scr_rows(self, r):
        return pl.ds(pl.multiple_of(self.scr0 + r * self.rc, self.rc), self.rc)

    def hbm_rows(self, r):
        return pl.ds(pl.multiple_of(self.hbm0 + r * self.rc, self.rc), self.rc)

    def x_copy(self, r, slot):
        return pltpu.make_async_copy(self.x_hbm.at[self.hbm_rows(r), :], self.xbuf.at[slot], self.sem_x.at[slot])

    def out_copy(self, r):
        return pltpu.make_async_copy(self.acc_scr.at[self.scr_rows(r), :], self.out_hbm.at[self.hbm_rows(r), :],
                                     self.sem_o.at[0])

    def hmix_copy(self, r):
        return pltpu.make_async_copy(self.hn_scr.at[self.scr_rows(r), :], self.hmix_hbm.at[self.hbm_rows(r), :],
                                     self.sem_h.at[0])

    def stream_x(self, process):
        self.x_copy(0, 0).start()

        def pair(j, carry):
            r = 2 * j
            self.x_copy(r + 1, 1).start()
            self.x_copy(r, 0).wait()
            process(r, 0)

            @pl.when(r + 2 < self.n_chunks)
            def _():
                self.x_copy(r + 2, 0).start()
            self.x_copy(r + 1, 1).wait()
            process(r + 1, 1)
            return carry
        lax.fori_loop(0, self.n_chunks // 2, pair, 0)

    def add_sumsq(self, part, v):
        sq = v * v
        for t in range(self.cw // LANES):
            part = part + sq[:, t * LANES:(t + 1) * LANES]
        return part

    def rstd_of(self, part):
        return lax.rsqrt(jnp.sum(part, axis=-1, keepdims=True) / self.d + EPS)

    def norm_mod(self, v, rstd, gain_ref, shift_row, scale_row, cols):
        y = v * rstd * gain_ref[:, cols]
        scale = self.mod_ref[scale_row:scale_row + 1, cols]
        return y * (1.0 + scale) + self.mod_ref[shift_row:shift_row + 1, cols]

    def prologue(self):
        def process(r, slot):
            rows = self.scr_rows(r)
            part = jnp.zeros((self.rc, LANES), F32)
            for cols in self.col_blocks:
                part = self.add_sumsq(part, self.xbuf[slot, :, cols])
            rstd = self.rstd_of(part)
            for cols in self.col_blocks:
                hn = self.norm_mod(self.xbuf[slot, :, cols], rstd, self.g_ref, 0, 1, cols)
                self.hn_scr[rows, cols] = hn.astype(BF16)
                self.acc_scr[rows, cols] = jnp.zeros((self.rc, self.cw), F32)
        self.stream_x(process)

    def epilogue(self):
        with_mix = self.hmix_hbm is not None
        write_out = self.out_hbm is not None

        def process(r, slot):
            rows = self.scr_rows(r)
            part = jnp.zeros((self.rc, LANES), F32)
            for cols in self.col_blocks:
                y = self.xbuf[slot, :, cols] + (0.5 * self.mod_ref[2:3, cols]) * self.acc_scr[rows, cols]
                self.acc_scr[rows, cols] = y
                part = self.add_sumsq(part, y)
            rstd = self.rstd_of(part)
            for cols in self.col_blocks:
                y = self.acc_scr[rows, cols]
                if with_mix:
                    self.hn_scr[rows, cols] = self.norm_mod(y, rstd, self.gmix_ref, 3, 4, cols).astype(BF16)
                if self.gfin_ref is not None:
                    self.acc_scr[rows, cols] = y * rstd * self.gfin_ref[:, cols]
            if with_mix:
                self.hmix_copy(r).start()
            if write_out:
                self.out_copy(r).start()
        self.stream_x(process)
        for _ in range(self.n_chunks):
            if with_mix:
                self.hmix_copy(0).wait()
            if write_out:
                self.out_copy(0).wait()


def _ffn_kernel(*refs, tm, n_side, n_f, with_mix, final_norm):
    refs = list(refs)
    x_hbm, mod_ref, g_ref, wg_ref, wu_ref, wd_ref = refs[:6]
    del refs[:6]
    gmix_ref = refs.pop(0) if with_mix else None
    gfin_ref = refs.pop(0) if final_norm else None
    xs_hbm, mods_ref = (refs.pop(0), refs.pop(0)) if n_side else (None, None)
    out_hbm = refs.pop(0)
    hmix_hbm = refs.pop(0) if with_mix else None
    hmixs_hbm = refs.pop(0) if n_side else None
    scratch = refs
    hn_scr, acc_scr = scratch[0], scratch[1]

    i = pl.program_id(0)
    f = pl.program_id(1)
    main = _RowGroup(x_hbm, mod_ref, g_ref, gmix_ref, gfin_ref, out_hbm, hmix_hbm, scratch,
                     scr0=0, hbm0=i * tm, n_rows=tm)
    side = _RowGroup(xs_hbm, mods_ref, g_ref, gmix_ref, None, None, hmixs_hbm, scratch,
                     scr0=tm, hbm0=0, n_rows=n_side) if n_side else None

    def swiglu(rows):
        hn = hn_scr[rows, :]
        gt = _bdot(hn, wg_ref)
        up = _bdot(hn, wu_ref)
        act = (_silu(gt) * up).astype(BF16)
        acc_scr[rows, :] += _bdot(act, wd_ref)

    @pl.when(f == 0)
    def _():
        main.prologue()

    if side is not None:
        @pl.when((f == 0) & (i == 0))
        def _():
            side.prologue()

    swiglu(slice(0, tm))

    if side is not None:
        @pl.when(i == 0)
        def _():
            swiglu(slice(tm, tm + n_side))

    @pl.when(f == n_f - 1)
    def _():
        main.epilogue()

    if side is not None:
        @pl.when((f == n_f - 1) & (i == 0))
        def _():
            side.epilogue()


def _ffn(x, mod, g, wg, wu, wd, *, g_mix=None, g_final=None, side=None):
    m, d = x.shape
    tm = min(FFN_ROWS, m)
    tf = FFN_COLS
    dff = wg.shape[1]
    assert m % tm == 0 and dff % tf == 0
    n_f = dff // tf
    with_mix = g_mix is not None
    final_norm = g_final is not None
    n_side = 0 if side is None else side[0].shape[0]
    assert not n_side or with_mix
    const = lambda i, f: (0, 0)
    hbm = pl.BlockSpec(memory_space=pl.ANY)
    in_specs = [
        hbm,
        pl.BlockSpec(mod.shape, const),
        pl.BlockSpec((1, d), const),
        pl.BlockSpec((d, tf), lambda i, f: (0, f)),
        pl.BlockSpec((d, tf), lambda i, f: (0, f)),
        pl.BlockSpec((tf, d), lambda i, f: (f, 0)),
    ]
    args = [x, mod, g.reshape(1, d), wg, wu, wd]
    if with_mix:
        in_specs.append(pl.BlockSpec((1, d), const))
        args.append(g_mix.reshape(1, d))
    if final_norm:
        in_specs.append(pl.BlockSpec((1, d), const))
        args.append(g_final.reshape(1, d))
    if n_side:
        in_specs += [hbm, pl.BlockSpec(side[1].shape, const)]
        args += [side[0], side[1]]
    out_shape = [jax.ShapeDtypeStruct((m, d), F32)]
    if with_mix:
        out_shape.append(jax.ShapeDtypeStruct((m, d), BF16))
    if n_side:
        out_shape.append(jax.ShapeDtypeStruct((n_side, d), BF16))
    res = pl.pallas_call(
        functools.partial(_ffn_kernel, tm=tm, n_side=n_side, n_f=n_f, with_mix=with_mix, final_norm=final_norm),
        grid=(m // tm, n_f),
        in_specs=in_specs,
        out_specs=[hbm] * len(out_shape),
        out_shape=out_shape,
        scratch_shapes=[
            pltpu.VMEM((tm + n_side, d), BF16),
            pltpu.VMEM((tm + n_side, d), F32),
            pltpu.VMEM((2, FFN_ROW_CHUNK, d), F32),
            pltpu.SemaphoreType.DMA((2,)),
            pltpu.SemaphoreType.DMA((1,)),
            pltpu.SemaphoreType.DMA((1,)),
        ],
        compiler_params=_params(("arbitrary", "arbitrary")),
        name="ffn_mix" if with_mix else "ffn_final",
    )(*args)
    return res if len(res) > 1 else res[0]


def _glu_kernel(a_ref, wu_ref, wg_ref, o_ref):
    a = a_ref[...]
    u = _bdot(a, wu_ref)
    g = _bdot(a, wg_ref)
    o_ref[...] = u * jax.nn.sigmoid(g)


def _glu_proj(a, w):
    m, d = a.shape
    tm = min(PROJ_ROWS, m)
    tn = GLU_COLS
    nb = D_CONV // tn
    return pl.pallas_call(
        _glu_kernel,
        grid=(m // tm, nb),
        in_specs=[
            pl.BlockSpec((tm, d), lambda i, j: (i, 0)),
            pl.BlockSpec((d, tn), lambda i, j: (0, j)),
            pl.BlockSpec((d, tn), lambda i, j: (0, j + nb)),
        ],
        out_specs=pl.BlockSpec((tm, tn), lambda i, j: (i, j)),
        out_shape=jax.ShapeDtypeStruct((m, D_CONV), F32),
        compiler_params=_params(("arbitrary", "arbitrary")),
        name="glu_proj",
    )(a, w, w)


def _project_norm_rope(a_ref, w_ref, gain_ref, cos_ref, sin_ref, o_ref):
    tm = a_ref.shape[0]
    part = tm // PROJ_ROW_SPLITS
    for s in range(PROJ_ROW_SPLITS):
        rows = slice(s * part, (s + 1) * part)
        p = _bdot(a_ref[rows, :], w_ref)
        cosf = cos_ref[rows, :]
        sinf = sin_ref[rows, :]
        even = (lax.broadcasted_iota(jnp.int32, cosf.shape, 1) % 2) == 0
        for h in range(p.shape[1] // HEAD_DIM):
            cols = slice(h * HEAD_DIM, (h + 1) * HEAD_DIM)
            ph = p[:, cols]
            ms = jnp.mean(ph * ph, axis=-1, keepdims=True)
            y = ph * lax.rsqrt(ms + EPS) * gain_ref[:, cols]
            partner = jnp.where(even, pltpu.roll(y, HEAD_DIM - 1, 1), pltpu.roll(y, 1, 1))
            o_ref[rows, cols] = (y * cosf + partner * sinf).astype(o_ref.dtype)


def _q_kernel(a_ref, w_ref, gain_ref, cos_ref, sin_ref, o_ref):
    _project_norm_rope(a_ref, w_ref, gain_ref, cos_ref, sin_ref, o_ref)


def _kv_kernel(a_ref, w_ref, gain_ref, cos_ref, sin_ref, k_ref, vt_ref):
    j = pl.program_id(1)

    @pl.when(j == 0)
    def _():
        _project_norm_rope(a_ref, w_ref, gain_ref, cos_ref, sin_ref, k_ref)

    @pl.when(j == 1)
    def _():
        tm = a_ref.shape[0]
        part = tm // PROJ_ROW_SPLITS
        for s in range(PROJ_ROW_SPLITS):
            rows = slice(s * part, (s + 1) * part)
            vt_ref[:, rows] = _bdot(a_ref[rows, :], w_ref).T.astype(vt_ref.dtype)


def _q_proj(a, w, gain, cosf, sinf, *, col0):
    m, d = a.shape
    tm = min(PROJ_ROWS, m)
    tn = QKV_COLS
    n = N_Q_HEADS * HEAD_DIM
    return pl.pallas_call(
        _q_kernel,
        grid=(m // tm, n // tn),
        in_specs=[
            pl.BlockSpec((tm, d), lambda i, j: (i, 0)),
            pl.BlockSpec((d, tn), lambda i, j: (0, j + col0 // tn)),
            pl.BlockSpec((1, tn), lambda i, j: (0, j)),
            pl.BlockSpec((tm, HEAD_DIM), lambda i, j: (i, 0)),
            pl.BlockSpec((tm, HEAD_DIM), lambda i, j: (i, 0)),
        ],
        out_specs=pl.BlockSpec((tm, tn), lambda i, j: (i, j)),
        out_shape=jax.ShapeDtypeStruct((m, n), BF16),
        compiler_params=_params(("arbitrary", "arbitrary")),
        name="q_proj",
    )(a, w, gain, cosf, sinf)


def _kv_proj(a, w, gain, cosf, sinf, *, col0):
    m, d = a.shape
    tm = min(PROJ_ROWS, m)
    n = N_KV_HEADS * HEAD_DIM
    assert n == QKV_COLS
    return pl.pallas_call(
        _kv_kernel,
        grid=(m // tm, 2),
        in_specs=[
            pl.BlockSpec((tm, d), lambda i, j: (i, 0)),
            pl.BlockSpec((d, n), lambda i, j: (0, j + col0 // n)),
            pl.BlockSpec((1, n), lambda i, j: (0, 0)),
            pl.BlockSpec((tm, HEAD_DIM), lambda i, j: (i, 0)),
            pl.BlockSpec((tm, HEAD_DIM), lambda i, j: (i, 0)),
        ],
        out_specs=[
            pl.BlockSpec((tm, n), lambda i, j: (i, 0)),
            pl.BlockSpec((n, tm), lambda i, j: (0, i)),
        ],
        out_shape=[jax.ShapeDtypeStruct((m, n), BF16), jax.ShapeDtypeStruct((n, m), BF16)],
        compiler_params=_params(("arbitrary", "arbitrary")),
        name="kv_proj",
    )(a, w, gain, cosf, sinf)


def _conv_kernel(prev_ref, main_ref, next_ref, w_ref, b_ref, lng_ref, lnb_ref, c_ref, wada_ref, bada_ref,
                 o_ref, mod_ref, e_scr, ph_scr, y_scr):
    _ada_kernel(c_ref, wada_ref, bada_ref, mod_ref)
    i = pl.program_id(0)
    n = pl.num_programs(0)
    ts, ch = main_ref.shape
    e_scr[0:CONV_HALO, :] = jnp.where(i > 0, prev_ref[...], 0.0)
    e_scr[CONV_HALO:CONV_HALO + ts, :] = main_ref[...]
    e_scr[CONV_HALO + ts:CONV_HALO + ts + CONV_HALO, :] = jnp.where(i < n - 1, next_ref[...], 0.0)
    first = CONV_HALO - CONV_WIDTH // 2
    ph_rows = ph_scr.shape[1]

    def lane_tile(c, carry):
        cols = pl.ds(pl.multiple_of(c * LANES, LANES), LANES)
        for ph in range(SUBLANES):
            ph_scr[ph] = e_scr[pl.ds(ph, ph_rows), cols]
        taps = [w_ref[k:k + 1, cols] for k in range(CONV_WIDTH)]
        bias = b_ref[:, cols]
        for r in range(ts // CONV_ROW_BLOCK):
            r0 = r * CONV_ROW_BLOCK
            acc = jnp.zeros((CONV_ROW_BLOCK, LANES), F32)
            for k in range(CONV_WIDTH):
                off = first + k
                acc = acc + ph_scr[off % SUBLANES, pl.ds(r0 + off - off % SUBLANES, CONV_ROW_BLOCK), :] * taps[k]
            y_scr[pl.ds(r0, CONV_ROW_BLOCK), cols] = acc + bias
        return carry
    lax.fori_loop(0, ch // LANES, lane_tile, 0)

    lb = min(LN_ROW_BLOCK, ts)

    def ln_rows(r, carry):
        rows = pl.ds(pl.multiple_of(r * lb, lb), lb)
        y = y_scr[rows, :]
        mu = jnp.mean(y, axis=-1, keepdims=True)
        yc = y - mu
        var = jnp.mean(yc * yc, axis=-1, keepdims=True)
        z = yc * lax.rsqrt(var + EPS) * lng_ref[...] + lnb_ref[...]
        o_ref[rows, :] = _silu(z).astype(o_ref.dtype)
        return carry
    lax.fori_loop(0, ts // lb, ln_rows, 0)


def _conv_module(glu, w, b, ln_g, ln_b, cc, w_ada, b_ada, ada_col0):
    l, ch = glu.shape
    ts = min(CONV_ROWS, l)
    hb = ts // CONV_HALO
    n = l // ts
    last_halo_block = l // CONV_HALO - 1
    mod_rows, d = cc.shape
    n_late = w_ada.shape[1] - ada_col0
    tn = n_late // n
    assert tn * n == n_late and tn % LANES == 0 and ada_col0 % tn == 0
    ada_blk0 = ada_col0 // tn
    ph_rows = ts + 2 * CONV_HALO - SUBLANES
    assert CONV_HALO - CONV_WIDTH // 2 + CONV_WIDTH - 1 + ts <= ph_rows + SUBLANES - 1
    return pl.pallas_call(
        _conv_kernel,
        grid=(n,),
        in_specs=[
            pl.BlockSpec((CONV_HALO, ch), lambda i: (jnp.maximum(i * hb - 1, 0), 0)),
            pl.BlockSpec((ts, ch), lambda i: (i, 0)),
            pl.BlockSpec((CONV_HALO, ch), lambda i: (jnp.minimum((i + 1) * hb, last_halo_block), 0)),
            pl.BlockSpec((CONV_WIDTH, ch), lambda i: (0, 0)),
            pl.BlockSpec((1, ch), lambda i: (0, 0)),
            pl.BlockSpec((1, ch), lambda i: (0, 0)),
            pl.BlockSpec((1, ch), lambda i: (0, 0)),
            pl.BlockSpec((mod_rows, d), lambda i: (0, 0)),
            pl.BlockSpec((d, tn), lambda i: (0, i + ada_blk0)),
            pl.BlockSpec((1, tn), lambda i: (0, i + ada_blk0)),
        ],
        out_specs=[
            pl.BlockSpec((ts, ch), lambda i: (i, 0)),
            pl.BlockSpec((mod_rows, tn), lambda i: (0, i)),
        ],
        out_shape=[
            jax.ShapeDtypeStruct((l, ch), BF16),
            jax.ShapeDtypeStruct((mod_rows, n_late), F32),
        ],
        scratch_shapes=[
            pltpu.VMEM((ts + 2 * CONV_HALO, ch), F32),
            pltpu.VMEM((SUBLANES, ph_rows, LANES), F32),
            pltpu.VMEM((ts, ch), F32),
        ],
        compiler_params=_params(("arbitrary",)),
        name="conv_module",
    )(glu, glu, glu, w, b.reshape(1, ch), ln_g.reshape(1, ch), ln_b.reshape(1, ch), cc, w_ada, b_ada)


def _attn_kernel(q_ref, kx_ref, vtx_ref, kc_ref, vtc_ref, o_ref, qs_scr, s0_scr, s1_scr, m_scr, l_scr, acc_scr,
                 klast_scr, vtlast_scr, *, tk, n_chunks):
    tq = q_ref.shape[0]
    part = GROUP * tq // ATTN_ROW_SPLITS
    last = n_chunks - 1

    @pl.when(pl.program_id(1) == 0)
    def _():
        n_lat = kx_ref.shape[0]
        rem = n_lat - last * tk
        if rem:
            klast_scr[0:rem, :] = kx_ref[n_lat - rem:n_lat, :]
            vtlast_scr[:, 0:rem] = vtx_ref[:, n_lat - rem:n_lat]
        klast_scr[rem:tk, :] = kc_ref[...]
        vtlast_scr[:, rem:tk] = vtc_ref[...]

    for g in range(GROUP):
        qs_scr[g * tq:(g + 1) * tq, :] = q_ref[:, g * HEAD_DIM:(g + 1) * HEAD_DIM]
    m_scr[...] = jnp.full(m_scr.shape, -jnp.inf, F32)
    l_scr[...] = jnp.zeros(l_scr.shape, F32)
    acc_scr[...] = jnp.zeros(acc_scr.shape, F32)

    def keys(c):
        return pl.ds(pl.multiple_of(c * tk, tk), tk)

    def is_last(c):
        return isinstance(c, int) and c == last

    def scores(c, s_scr):
        k = klast_scr[...] if is_last(c) else kx_ref[keys(c), :]
        for h in range(ATTN_ROW_SPLITS):
            r = slice(h * part, (h + 1) * part)
            s_scr[:, r] = lax.dot_general(k, qs_scr[r, :], (((1,), (1,)), ((), ())),
                                          preferred_element_type=F32)

    def accumulate(c, s_scr):
        vt = vtlast_scr[...] if is_last(c) else vtx_ref[:, keys(c)]
        for h in range(ATTN_ROW_SPLITS):
            r = slice(h * part, (h + 1) * part)
            s = s_scr[:, r]
            m_prev = m_scr[:, r]
            m_new = jnp.maximum(m_prev, jnp.max(s, axis=0, keepdims=True))
            alpha = jnp.exp2(m_prev - m_new)
            p = jnp.exp2(s - m_new)
            l_scr[:, r] = alpha * l_scr[:, r] + jnp.sum(p, axis=0, keepdims=True)
            acc_scr[:, r] = alpha * acc_scr[:, r] + jnp.dot(vt, p.astype(BF16), preferred_element_type=F32)
            m_scr[:, r] = m_new

    bufs = (s0_scr, s1_scr)
    scores(0, s0_scr)
    n_pairs = max((n_chunks - 2) // 2, 0)

    def pair(j, carry):
        scores(2 * j + 1, s1_scr)
        accumulate(2 * j, s0_scr)
        scores(2 * j + 2, s0_scr)
        accumulate(2 * j + 1, s1_scr)
        return carry
    lax.fori_loop(0, n_pairs, pair, 0)
    for c in range(2 * n_pairs, n_chunks):
        if c + 1 < n_chunks:
            scores(c + 1, bufs[(c + 1) % 2])
        accumulate(c, bufs[c % 2])

    o_t = acc_scr[...] / l_scr[...]
    for g in range(GROUP):
        o_ref[:, g * HEAD_DIM:(g + 1) * HEAD_DIM] = o_t[:, g * tq:(g + 1) * tq].T.astype(o_ref.dtype)


def _key_chunk(n_keys):
    fits = [t for t in range(LANES, ATTN_MAX_KEY_CHUNK + 1, LANES) if n_keys % t == 0]
    assert fits, n_keys
    return fits[-1]


def _attention(q, k_x, vt_x, k_c, vt_c):
    l = q.shape[0]
    n_ctx = k_c.shape[0]
    n_keys = l + n_ctx
    tq = min(ATTN_Q_ROWS, l)
    tk = _key_chunk(n_keys)
    n_chunks = n_keys // tk
    rem = l - (n_chunks - 1) * tk
    assert 0 <= rem and rem + n_ctx == tk and rem % LANES == 0
    gw = GROUP * HEAD_DIM
    m = GROUP * tq
    return pl.pallas_call(
        functools.partial(_attn_kernel, tk=tk, n_chunks=n_chunks),
        grid=(N_KV_HEADS, l // tq),
        in_specs=[
            pl.BlockSpec((tq, gw), lambda h, i: (i, h)),
            pl.BlockSpec((l, HEAD_DIM), lambda h, i: (0, h)),
            pl.BlockSpec((HEAD_DIM, l), lambda h, i: (h, 0)),
            pl.BlockSpec((n_ctx, HEAD_DIM), lambda h, i: (0, h)),
            pl.BlockSpec((HEAD_DIM, n_ctx), lambda h, i: (h, 0)),
        ],
        out_specs=pl.BlockSpec((tq, gw), lambda h, i: (i, h)),
        out_shape=jax.ShapeDtypeStruct((l, N_Q_HEADS * HEAD_DIM), BF16),
        scratch_shapes=[
            pltpu.VMEM((m, HEAD_DIM), BF16),
            pltpu.VMEM((tk, m), F32),
            pltpu.VMEM((tk, m), F32),
            pltpu.VMEM((1, m), F32),
            pltpu.VMEM((1, m), F32),
            pltpu.VMEM((HEAD_DIM, m), F32),
            pltpu.VMEM((tk, HEAD_DIM), BF16),
            pltpu.VMEM((HEAD_DIM, tk), BF16),
        ],
        compiler_params=_params(("arbitrary", "arbitrary")),
        name="gqa_attention",
    )(q, k_x, vt_x, k_c, vt_c)


def _outproj_kernel(conv_ref, attn_ref, wc_ref, wa_ref, x_ref, gate_ref, o_ref):
    mix = _bdot(conv_ref[...], wc_ref) + _bdot(attn_ref[...], wa_ref)
    o_ref[...] = x_ref[...] + gate_ref[...] * mix


def _out_proj(conv_x, attn_x, w, x, gate):
    m, d = x.shape
    tm = min(PROJ_ROWS, m)
    tn = OUT_COLS
    kc = conv_x.shape[1]
    ka = attn_x.shape[1]
    assert kc == ka
    return pl.pallas_call(
        _outproj_kernel,
        grid=(m // tm, d // tn),
        in_specs=[
            pl.BlockSpec((tm, kc), lambda i, j: (i, 0)),
            pl.BlockSpec((tm, ka), lambda i, j: (i, 0)),
            pl.BlockSpec((kc, tn), lambda i, j: (0, j)),
            pl.BlockSpec((ka, tn), lambda i, j: (1, j)),
            pl.BlockSpec((tm, tn), lambda i, j: (i, j)),
            pl.BlockSpec((1, tn), lambda i, j: (0, j)),
        ],
        out_specs=pl.BlockSpec((tm, tn), lambda i, j: (i, j)),
        out_shape=jax.ShapeDtypeStruct((m, d), F32),
        compiler_params=_params(("arbitrary", "arbitrary")),
        name="out_proj",
    )(conv_x, attn_x, w, w, x, gate)


def _rope_tables(l):
    pos = jnp.arange(l, dtype=jnp.int32)
    row = (pos // GRID_W).astype(F32)
    col = (pos % GRID_W).astype(F32)
    axis_dim = HEAD_DIM // 2
    inv = ROPE_THETA ** (-jnp.arange(0, axis_dim, 2, dtype=F32) / axis_dim)
    ang = jnp.concatenate([row[:, None] * inv, col[:, None] * inv], axis=-1)
    cos, sin = jnp.cos(ang), jnp.sin(ang)
    cosf = jnp.repeat(cos, 2, axis=-1)
    sinf = jnp.stack([-sin, sin], axis=-1).reshape(l, HEAD_DIM)
    return cosf, sinf


def kernel(x, c, ctx, c_ctx, w_ada, b_ada, g_ffn1, w1_gate, w1_up, w1_down, g_mix, w_in, conv_w, conv_b,
           conv_ln_g, conv_ln_b, q_norm_g, k_norm_g, w_out, g_ffn2, w2_gate, w2_up, w2_down, g_final):
    b, l, d = x.shape
    depth = w_ada.shape[0]
    assert b == 1 and depth == 1 and l % GRID_W == 0
    n_ctx = ctx.shape[1]
    xs = x[0]
    cs = ctx[0]

    cc = jnp.zeros((SUBLANES, d), F32).at[0].set(c[0]).at[1].set(c_ctx)
    n_early = 5 * d
    b_ada_row = b_ada[0].reshape(1, -1)
    mods = _ada(cc, w_ada[0], b_ada_row, n_early)
    mx = mods[0].reshape(-1, d)
    mc = mods[1].reshape(-1, d)

    x1, hx, hc = _ffn(xs, mx[0:5], g_ffn1[0], w1_gate[0], w1_up[0], w1_down[0], g_mix=g_mix[0],
                      side=(cs, mc[0:5]))

    win = w_in[0]
    glu = _glu_proj(hx, win)
    q_gain = jnp.tile(q_norm_g[0] * (ATTN_SCALE * LOG2E), N_Q_HEADS).reshape(1, -1)
    k_gain = jnp.tile(k_norm_g[0], N_KV_HEADS).reshape(1, -1)
    cosf, sinf = _rope_tables(l)
    col_q = 2 * D_CONV
    col_k = col_q + N_Q_HEADS * HEAD_DIM
    q = _q_proj(hx, win, q_gain, cosf, sinf, col0=col_q)
    k_x, vt_x = _kv_proj(hx, win, k_gain, cosf, sinf, col0=col_k)
    k_c, vt_c = _kv_proj(hc, win, k_gain, jnp.ones((n_ctx, HEAD_DIM), F32), jnp.zeros((n_ctx, HEAD_DIM), F32),
                         col0=col_k)

    conv_x, mods_late = _conv_module(glu, conv_w[0], conv_b[0], conv_ln_g[0], conv_ln_b[0],
                                     cc, w_ada[0], b_ada_row, n_early)
    mx_late = mods_late[0].reshape(-1, d)
    attn_x = _attention(q, k_x, vt_x, k_c, vt_c)

    x2 = _out_proj(conv_x, attn_x, w_out[0], x1, mx_late[0:1])

    out = _ffn(x2, mx_late[1:4], g_ffn2[0], w2_gate[0], w2_up[0], w2_down[0], g_final=g_final)
    return out[None]
```

```python
import functools
import math

import jax
import jax.numpy as jnp
from jax import lax
from jax.experimental import pallas as pl
from jax.experimental.pallas import tpu as pltpu

F32 = jnp.float32
BF16 = jnp.bfloat16

EPS = 1e-6
HEAD_DIM = 128
N_Q_HEADS = 16
N_KV_HEADS = 4
GROUP = N_Q_HEADS // N_KV_HEADS
D_CONV = 2048
CONV_WIDTH = 31
GRID_W = 64
ROPE_THETA = 10000.0
LOG2E = math.log2(math.e)
ATTN_SCALE = 1.0 / math.sqrt(HEAD_DIM)

V7X_VMEM_BYTES = 64 * 1024 * 1024
VMEM_LIMIT_BYTES = V7X_VMEM_BYTES - 4 * 1024 * 1024
LANES = 128
SUBLANES = 8
MXU_WIDTH = 256

FFN_ROWS = 1024
FFN_COLS = MXU_WIDTH
FFN_ROW_CHUNK = 64
FFN_COL_BLOCK = 512
ADA_COLS = 512
PROJ_ROWS = 1024
PROJ_ROW_SPLITS = 4
GLU_COLS = 256
QKV_COLS = 512
OUT_COLS = 512
CONV_ROWS = 256
CONV_HALO = 16
CONV_ROW_BLOCK = 32
LN_ROW_BLOCK = 128
ATTN_Q_ROWS = 256
ATTN_ROW_SPLITS = 2
ATTN_MAX_KEY_CHUNK = 1408


def _params(semantics):
    return pltpu.CompilerParams(dimension_semantics=semantics, vmem_limit_bytes=VMEM_LIMIT_BYTES)


def _silu(v):
    return v * jax.nn.sigmoid(v)


def _bdot(a, w_ref):
    return jnp.dot(a, w_ref[...].astype(BF16), preferred_element_type=F32)


def _ada_kernel(c_ref, w_ref, b_ref, o_ref):
    a = _silu(c_ref[...]).astype(BF16)
    o_ref[...] = _bdot(a, w_ref) + b_ref[...]


def _ada(cc, w, b, n_cols):
    rows, d = cc.shape
    tn = ADA_COLS
    assert n_cols % tn == 0
    return pl.pallas_call(
        _ada_kernel,
        grid=(n_cols // tn,),
        in_specs=[
            pl.BlockSpec((rows, d), lambda j: (0, 0)),
            pl.BlockSpec((d, tn), lambda j: (0, j)),
            pl.BlockSpec((1, tn), lambda j: (0, j)),
        ],
        out_specs=pl.BlockSpec((rows, tn), lambda j: (0, j)),
        out_shape=jax.ShapeDtypeStruct((rows, n_cols), F32),
        compiler_params=_params(("arbitrary",)),
        name="ada_mod",
    )(cc, w, b)


class _RowGroup:
    def __init__(self, x_hbm, mod_ref, g_ref, gmix_ref, gfin_ref, out_hbm, hmix_hbm, scratch, *, scr0, hbm0, n_rows):
        self.x_hbm, self.mod_ref, self.g_ref, self.gmix_ref, self.gfin_ref = x_hbm, mod_ref, g_ref, gmix_ref, gfin_ref
        self.out_hbm, self.hmix_hbm = out_hbm, hmix_hbm
        self.hn_scr, self.acc_scr, self.xbuf, self.sem_x, self.sem_o, self.sem_h = scratch
        self.scr0, self.hbm0 = scr0, hbm0
        self.rc = FFN_ROW_CHUNK
        self.n_chunks = n_rows // self.rc
        assert self.n_chunks * self.rc == n_rows and self.n_chunks % 2 == 0
        self.d = self.acc_scr.shape[1]
        self.cw = FFN_COL_BLOCK
        self.col_blocks = [slice(c0, c0 + self.cw) for c0 in range(0, self.d, self.cw)]

    def scr_rows(self, r):
        return pl.ds(pl.multiple_of(self.scr0 + r * self.rc, self.rc), self.rc)

    def hbm_rows(self, r):
        return pl.ds(pl.multiple_of(self.hbm0 + r * self.rc, self.rc), self.rc)

    def x_copy(self, r, slot):
        return pltpu.make_async_copy(self.x_hbm.at[self.hbm_rows(r), :], self.xbuf.at[slot], self.sem_x.at[slot])

    def out_copy(self, r):
        return pltpu.make_async_copy(self.acc_scr.at[self.scr_rows(r), :], self.out_hbm.at[self.hbm_rows(r), :],
                                     self.sem_o.at[0])

    def hmix_copy(self, r):
        return pltpu.make_async_copy(self.hn_scr.at[self.scr_rows(r), :], self.hmix_hbm.at[self.hbm_rows(r), :],
                                     self.sem_h.at[0])

    def stream_x(self, process):
        self.x_copy(0, 0).start()

        def pair(j, carry):
            r = 2 * j
            self.x_copy(r + 1, 1).start()
            self.x_copy(r, 0).wait()
            process(r, 0)

            @pl.when(r + 2 < self.n_chunks)
            def _():
                self.x_copy(r + 2, 0).start()
            self.x_copy(r + 1, 1).wait()
            process(r + 1, 1)
            return carry
        lax.fori_loop(0, self.n_chunks // 2, pair, 0)

    def add_sumsq(self, part, v):
        sq = v * v
        for t in range(self.cw // LANES):
            part = part + sq[:, t * LANES:(t + 1) * LANES]
        return part

    def rstd_of(self, part):
        return lax.rsqrt(jnp.sum(part, axis=-1, keepdims=True) / self.d + EPS)

    def norm_mod(self, v, rstd, gain_ref, shift_row, scale_row, cols):
        y = v * rstd * gain_ref[:, cols]
        scale = self.mod_ref[scale_row:scale_row + 1, cols]
        return y * (1.0 + scale) + self.mod_ref[shift_row:shift_row + 1, cols]

    def prologue(self):
        def process(r, slot):
            rows = self.scr_rows(r)
            part = jnp.zeros((self.rc, LANES), F32)
            for cols in self.col_blocks:
                part = self.add_sumsq(part, self.xbuf[slot, :, cols])
            rstd = self.rstd_of(part)
            for cols in self.col_blocks:
                hn = self.norm_mod(self.xbuf[slot, :, cols], rstd, self.g_ref, 0, 1, cols)
                self.hn_scr[rows, cols] = hn.astype(BF16)
                self.acc_scr[rows, cols] = jnp.zeros((self.rc, self.cw), F32)
        self.stream_x(process)

    def epilogue(self):
        with_mix = self.hmix_hbm is not None
        write_out = self.out_hbm is not None

        def process(r, slot):
            rows = self.scr_rows(r)
            part = jnp.zeros((self.rc, LANES), F32)
            for cols in self.col_blocks:
                y = self.xbuf[slot, :, cols] + (0.5 * self.mod_ref[2:3, cols]) * self.acc_scr[rows, cols]
                self.acc_scr[rows, cols] = y
                part = self.add_sumsq(part, y)
            rstd = self.rstd_of(part)
            for cols in self.col_blocks:
                y = self.acc_scr[rows, cols]
                if with_mix:
                    self.hn_scr[rows, cols] = self.norm_mod(y, rstd, self.gmix_ref, 3, 4, cols).astype(BF16)
                if self.gfin_ref is not None:
                    self.acc_scr[rows, cols] = y * rstd * self.gfin_ref[:, cols]
            if with_mix:
                self.hmix_copy(r).start()
            if write_out:
                self.out_copy(r).start()
        self.stream_x(process)
        for _ in range(self.n_chunks):
            if with_mix:
                self.hmix_copy(0).wait()
            if write_out:
                self.out_copy(0).wait()


def _ffn_kernel(*refs, tm, n_side, n_f, with_mix, final_norm):
    refs = list(refs)
    x_hbm, mod_ref, g_ref, wg_ref, wu_ref, wd_ref = refs[:6]
    del refs[:6]
    gmix_ref = refs.pop(0) if with_mix else None
    gfin_ref = refs.pop(0) if final_norm else None
    xs_hbm, mods_ref = (refs.pop(0), refs.pop(0)) if n_side else (None, None)
    out_hbm = refs.pop(0)
    hmix_hbm = refs.pop(0) if with_mix else None
    hmixs_hbm = refs.pop(0) if n_side else None
    scratch = refs
    hn_scr, acc_scr = scratch[0], scratch[1]

    i = pl.program_id(0)
    f = pl.program_id(1)
    main = _RowGroup(x_hbm, mod_ref, g_ref, gmix_ref, gfin_ref, out_hbm, hmix_hbm, scratch,
                     scr0=0, hbm0=i * tm, n_rows=tm)
    side = _RowGroup(xs_hbm, mods_ref, g_ref, gmix_ref, None, None, hmixs_hbm, scratch,
                     scr0=tm, hbm0=0, n_rows=n_side) if n_side else None

    def swiglu(rows):
        hn = hn_scr[rows, :]
        gt = _bdot(hn, wg_ref)
        up = _bdot(hn, wu_ref)
        act = (_silu(gt) * up).astype(BF16)
        acc_scr[rows, :] += _bdot(act, wd_ref)

    @pl.when(f == 0)
    def _():
        main.prologue()

    if side is not None:
        @pl.when((f == 0) & (i == 0))
        def _():
            side.prologue()

    swiglu(slice(0, tm))

    if side is not None:
        @pl.when(i == 0)
        def _():
            swiglu(slice(tm, tm + n_side))

    @pl.when(f == n_f - 1)
    def _():
        main.epilogue()

    if side is not None:
        @pl.when((f == n_f - 1) & (i == 0))
        def _():
            side.epilogue()


def _ffn(x, mod, g, wg, wu, wd, *, g_mix=None, g_final=None, side=None):
    m, d = x.shape
    tm = min(FFN_ROWS, m)
    tf = FFN_COLS
    dff = wg.shape[1]
    assert m % tm == 0 and dff % tf == 0
    n_f = dff // tf
    with_mix = g_mix is not None
    final_norm = g_final is not None
    n_side = 0 if side is None else side[0].shape[0]
    assert not n_side or with_mix
    const = lambda i, f: (0, 0)
    hbm = pl.BlockSpec(memory_space=pl.ANY)
    in_specs = [
        hbm,
        pl.BlockSpec(mod.shape, const),
        pl.BlockSpec((1, d), const),
        pl.BlockSpec((d, tf), lambda i, f: (0, f)),
        pl.BlockSpec((d, tf), lambda i, f: (0, f)),
        pl.BlockSpec((tf, d), lambda i, f: (f, 0)),
    ]
    args = [x, mod, g.reshape(1, d), wg, wu, wd]
    if with_mix:
        in_specs.append(pl.BlockSpec((1, d), const))
        args.append(g_mix.reshape(1, d))
    if final_norm:
        in_specs.append(pl.BlockSpec((1, d), const))
        args.append(g_final.reshape(1, d))
    if n_side:
        in_specs += [hbm, pl.BlockSpec(side[1].shape, const)]
        args += [side[0], side[1]]
    out_shape = [jax.ShapeDtypeStruct((m, d), F32)]
    if with_mix:
        out_shape.append(jax.ShapeDtypeStruct((m, d), BF16))
    if n_side:
        out_shape.append(jax.ShapeDtypeStruct((n_side, d), BF16))
    res = pl.pallas_call(
        functools.partial(_ffn_kernel, tm=tm, n_side=n_side, n_f=n_f, with_mix=with_mix, final_norm=final_norm),
        grid=(m // tm, n_f),
        in_specs=in_specs,
        out_specs=[hbm] * len(out_shape),
        out_shape=out_shape,
        scratch_shapes=[
            pltpu.VMEM((tm + n_side, d), BF16),
            pltpu.VMEM((tm + n_side, d), F32),
            pltpu.VMEM((2, FFN_ROW_CHUNK, d), F32),
            pltpu.SemaphoreType.DMA((2,)),
            pltpu.SemaphoreType.DMA((1,)),
            pltpu.SemaphoreType.DMA((1,)),
        ],
        compiler_params=_params(("arbitrary", "arbitrary")),
        name="ffn_mix" if with_mix else "ffn_final",
    )(*args)
    return res if len(res) > 1 else res[0]


def _glu_kernel(a_ref, wu_ref, wg_ref, o_ref):
    a = a_ref[...]
    u = _bdot(a, wu_ref)
    g = _bdot(a, wg_ref)
    o_ref[...] = u * jax.nn.sigmoid(g)


def _glu_proj(a, w):
    m, d = a.shape
    tm = min(PROJ_ROWS, m)
    tn = GLU_COLS
    nb = D_CONV // tn
    return pl.pallas_call(
        _glu_kernel,
        grid=(m // tm, nb),
        in_specs=[
            pl.BlockSpec((tm, d), lambda i, j: (i, 0)),
            pl.BlockSpec((d, tn), lambda i, j: (0, j)),
            pl.BlockSpec((d, tn), lambda i, j: (0, j + nb)),
        ],
        out_specs=pl.BlockSpec((tm, tn), lambda i, j: (i, j)),
        out_shape=jax.ShapeDtypeStruct((m, D_CONV), F32),
        compiler_params=_params(("arbitrary", "arbitrary")),
        name="glu_proj",
    )(a, w, w)


def _project_norm_rope(a_ref, w_ref, gain_ref, cos_ref, sin_ref, o_ref):
    tm = a_ref.shape[0]
    part = tm // PROJ_ROW_SPLITS
    for s in range(PROJ_ROW_SPLITS):
        rows = slice(s * part, (s + 1) * part)
        p = _bdot(a_ref[rows, :], w_ref)
        cosf = cos_ref[rows, :]
        sinf = sin_ref[rows, :]
        even = (lax.broadcasted_iota(jnp.int32, cosf.shape, 1) % 2) == 0
        for h in range(p.shape[1] // HEAD_DIM):
            cols = slice(h * HEAD_DIM, (h + 1) * HEAD_DIM)
            ph = p[:, cols]
            ms = jnp.mean(ph * ph, axis=-1, keepdims=True)
            y = ph * lax.rsqrt(ms + EPS) * gain_ref[:, cols]
            partner = jnp.where(even, pltpu.roll(y, HEAD_DIM - 1, 1), pltpu.roll(y, 1, 1))
            o_ref[rows, cols] = (y * cosf + partner * sinf).astype(o_ref.dtype)


def _q_kernel(a_ref, w_ref, gain_ref, cos_ref, sin_ref, o_ref):
    _project_norm_rope(a_ref, w_ref, gain_ref, cos_ref, sin_ref, o_ref)


def _kv_kernel(a_ref, w_ref, gain_ref, cos_ref, sin_ref, k_ref, vt_ref):
    j = pl.program_id(1)

    @pl.when(j == 0)
    def _():
        _project_norm_rope(a_ref, w_ref, gain_ref, cos_ref, sin_ref, k_ref)

    @pl.when(j == 1)
    def _():
        tm = a_ref.shape[0]
        part = tm // PROJ_ROW_SPLITS
        for s in range(PROJ_ROW_SPLITS):
            rows = slice(s * part, (s + 1) * part)
            vt_ref[:, rows] = _bdot(a_ref[rows, :], w_ref).T.astype(vt_ref.dtype)


def _q_proj(a, w, gain, cosf, sinf, *, col0):
    m, d = a.shape
    tm = min(PROJ_ROWS, m)
    tn = QKV_COLS
    n = N_Q_HEADS * HEAD_DIM
    return pl.pallas_call(
        _q_kernel,
        grid=(m // tm, n // tn),
        in_specs=[
            pl.BlockSpec((tm, d), lambda i, j: (i, 0)),
            pl.BlockSpec((d, tn), lambda i, j: (0, j + col0 // tn)),
            pl.BlockSpec((1, tn), lambda i, j: (0, j)),
            pl.BlockSpec((tm, HEAD_DIM), lambda i, j: (i, 0)),
            pl.BlockSpec((tm, HEAD_DIM), lambda i, j: (i, 0)),
        ],
        out_specs=pl.BlockSpec((tm, tn), lambda i, j: (i, j)),
        out_shape=jax.ShapeDtypeStruct((m, n), BF16),
        compiler_params=_params(("arbitrary", "arbitrary")),
        name="q_proj",
    )(a, w, gain, cosf, sinf)


def _kv_proj(a, w, gain, cosf, sinf, *, col0):
    m, d = a.shape
    tm = min(PROJ_ROWS, m)
    n = N_KV_HEADS * HEAD_DIM
    assert n == QKV_COLS
    return pl.pallas_call(
        _kv_kernel,
        grid=(m // tm, 2),
        in_specs=[
            pl.BlockSpec((tm, d), lambda i, j: (i, 0)),
            pl.BlockSpec((d, n), lambda i, j: (0, j + col0 // n)),
            pl.BlockSpec((1, n), lambda i, j: (0, 0)),
            pl.BlockSpec((tm, HEAD_DIM), lambda i, j: (i, 0)),
            pl.BlockSpec((tm, HEAD_DIM), lambda i, j: (i, 0)),
        ],
        out_specs=[
            pl.BlockSpec((tm, n), lambda i, j: (i, 0)),
            pl.BlockSpec((n, tm), lambda i, j: (0, i)),
        ],
        out_shape=[jax.ShapeDtypeStruct((m, n), BF16), jax.ShapeDtypeStruct((n, m), BF16)],
        compiler_params=_params(("arbitrary", "arbitrary")),
        name="kv_proj",
    )(a, w, gain, cosf, sinf)


def _conv_kernel(prev_ref, main_ref, next_ref, w_ref, b_ref, lng_ref, lnb_ref, c_ref, wada_ref, bada_ref,
                 o_ref, mod_ref, e_scr, ph_scr, y_scr):
    _ada_kernel(c_ref, wada_ref, bada_ref, mod_ref)
    i = pl.program_id(0)
    n = pl.num_programs(0)
    ts, ch = main_ref.shape
    e_scr[0:CONV_HALO, :] = jnp.where(i > 0, prev_ref[...], 0.0)
    e_scr[CONV_HALO:CONV_HALO + ts, :] = main_ref[...]
    e_scr[CONV_HALO + ts:CONV_HALO + ts + CONV_HALO, :] = jnp.where(i < n - 1, next_ref[...], 0.0)
    first = CONV_HALO - CONV_WIDTH // 2
    ph_rows = ph_scr.shape[1]

    def lane_tile(c, carry):
        cols = pl.ds(pl.multiple_of(c * LANES, LANES), LANES)
        for ph in range(SUBLANES):
            ph_scr[ph] = e_scr[pl.ds(ph, ph_rows), cols]
        taps = [w_ref[k:k + 1, cols] for k in range(CONV_WIDTH)]
        bias = b_ref[:, cols]
        for r in range(ts // CONV_ROW_BLOCK):
            r0 = r * CONV_ROW_BLOCK
            acc = jnp.zeros((CONV_ROW_BLOCK, LANES), F32)
            for k in range(CONV_WIDTH):
                off = first + k
                acc = acc + ph_scr[off % SUBLANES, pl.ds(r0 + off - off % SUBLANES, CONV_ROW_BLOCK), :] * taps[k]
            y_scr[pl.ds(r0, CONV_ROW_BLOCK), cols] = acc + bias
        return carry
    lax.fori_loop(0, ch // LANES, lane_tile, 0)

    lb = min(LN_ROW_BLOCK, ts)

    def ln_rows(r, carry):
        rows = pl.ds(pl.multiple_of(r * lb, lb), lb)
        y = y_scr[rows, :]
        mu = jnp.mean(y, axis=-1, keepdims=True)
        yc = y - mu
        var = jnp.mean(yc * yc, axis=-1, keepdims=True)
        z = yc * lax.rsqrt(var + EPS) * lng_ref[...] + lnb_ref[...]
        o_ref[rows, :] = _silu(z).astype(o_ref.dtype)
        return carry
    lax.fori_loop(0, ts // lb, ln_rows, 0)


def _conv_module(glu, w, b, ln_g, ln_b, cc, w_ada, b_ada, ada_col0):
    l, ch = glu.shape
    ts = min(CONV_ROWS, l)
    hb = ts // CONV_HALO
    n = l // ts
    last_halo_block = l // CONV_HALO - 1
    mod_rows, d = cc.shape
    n_late = w_ada.shape[1] - ada_col0
    tn = n_late // n
    assert tn * n == n_late and tn % LANES == 0 and ada_col0 % tn == 0
    ada_blk0 = ada_col0 // tn
    ph_rows = ts + 2 * CONV_HALO - SUBLANES
    assert CONV_HALO - CONV_WIDTH // 2 + CONV_WIDTH - 1 + ts <= ph_rows + SUBLANES - 1
    return pl.pallas_call(
        _conv_kernel,
        grid=(n,),
        in_specs=[
            pl.BlockSpec((CONV_HALO, ch), lambda i: (jnp.maximum(i * hb - 1, 0), 0)),
            pl.BlockSpec((ts, ch), lambda i: (i, 0)),
            pl.BlockSpec((CONV_HALO, ch), lambda i: (jnp.minimum((i + 1) * hb, last_halo_block), 0)),
            pl.BlockSpec((CONV_WIDTH, ch), lambda i: (0, 0)),
            pl.BlockSpec((1, ch), lambda i: (0, 0)),
            pl.BlockSpec((1, ch), lambda i: (0, 0)),
            pl.BlockSpec((1, ch), lambda i: (0, 0)),
            pl.BlockSpec((mod_rows, d), lambda i: (0, 0)),
            pl.BlockSpec((d, tn), lambda i: (0, i + ada_blk0)),
            pl.BlockSpec((1, tn), lambda i: (0, i + ada_blk0)),
        ],
        out_specs=[
            pl.BlockSpec((ts, ch), lambda i: (i, 0)),
            pl.BlockSpec((mod_rows, tn), lambda i: (0, i)),
        ],
        out_shape=[
            jax.ShapeDtypeStruct((l, ch), BF16),
            jax.ShapeDtypeStruct((mod_rows, n_late), F32),
        ],
        scratch_shapes=[
            pltpu.VMEM((ts + 2 * CONV_HALO, ch), F32),
            pltpu.VMEM((SUBLANES, ph_rows, LANES), F32),
            pltpu.VMEM((ts, ch), F32),
        ],
        compiler_params=_params(("arbitrary",)),
        name="conv_module",
    )(glu, glu, glu, w, b.reshape(1, ch), ln_g.reshape(1, ch), ln_b.reshape(1, ch), cc, w_ada, b_ada)


def _attn_kernel(q_ref, qn_ref, kx_ref, vtx_ref, kc_ref, vtc_ref, o_ref, qs_scr, qsn_scr, s0_scr, s1_scr,
                 m_scr, l_scr, acc_scr, klast_scr, vtlast_scr, *, tk, n_chunks):
    tq = q_ref.shape[0]
    part = GROUP * tq // ATTN_ROW_SPLITS
    last = n_chunks - 1
    i = pl.program_id(1)

    def keys(c):
        return pl.ds(pl.multiple_of(c * tk, tk), tk)

    def is_last(c):
        return isinstance(c, int) and c == last

    def stack_heads(src_ref, dst_scr):
        for g in range(GROUP):
            dst_scr[g * tq:(g + 1) * tq, :] = src_ref[:, g * HEAD_DIM:(g + 1) * HEAD_DIM]

    def scores(c, s_scr, q_scr=qs_scr):
        k = klast_scr[...] if is_last(c) else kx_ref[keys(c), :]
        for h in range(ATTN_ROW_SPLITS):
            r = slice(h * part, (h + 1) * part)
            s_scr[:, r] = lax.dot_general(k, q_scr[r, :], (((1,), (1,)), ((), ())),
                                          preferred_element_type=F32)

    @pl.when(i == 0)
    def _():
        n_lat = kx_ref.shape[0]
        rem = n_lat - last * tk
        if rem:
            klast_scr[0:rem, :] = kx_ref[n_lat - rem:n_lat, :]
            vtlast_scr[:, 0:rem] = vtx_ref[:, n_lat - rem:n_lat]
        klast_scr[rem:tk, :] = kc_ref[...]
        vtlast_scr[:, rem:tk] = vtc_ref[...]
        stack_heads(q_ref, qs_scr)
        scores(0, s0_scr)

    @pl.when(i > 0)
    def _():
        qs_scr[...] = qsn_scr[...]

    m_scr[...] = jnp.full(m_scr.shape, -jnp.inf, F32)
    l_scr[...] = jnp.zeros(l_scr.shape, F32)
    acc_scr[...] = jnp.zeros(acc_scr.shape, F32)

    def prefetch_next_tile():
        stack_heads(qn_ref, qsn_scr)
        scores(0, s0_scr, qsn_scr)

    def accumulate(c, s_scr):
        vt = vtlast_scr[...] if is_last(c) else vtx_ref[:, keys(c)]
        for h in range(ATTN_ROW_SPLITS):
            r = slice(h * part, (h + 1) * part)
            s = s_scr[:, r]
            m_prev = m_scr[:, r]
            m_new = jnp.maximum(m_prev, jnp.max(s, axis=0, keepdims=True))
            alpha = jnp.exp2(m_prev - m_new)
            p = jnp.exp2(s - m_new)
            l_scr[:, r] = alpha * l_scr[:, r] + jnp.sum(p, axis=0, keepdims=True)
            acc_scr[:, r] = alpha * acc_scr[:, r] + jnp.dot(vt, p.astype(BF16), preferred_element_type=F32)
            m_scr[:, r] = m_new

    bufs = (s0_scr, s1_scr)
    n_pairs = max((n_chunks - 2) // 2, 0)

    def pair(j, carry):
        scores(2 * j + 1, s1_scr)
        accumulate(2 * j, s0_scr)
        scores(2 * j + 2, s0_scr)
        accumulate(2 * j + 1, s1_scr)
        return carry
    lax.fori_loop(0, n_pairs, pair, 0)
    for c in range(2 * n_pairs, n_chunks):
        if c + 1 < n_chunks:
            scores(c + 1, bufs[(c + 1) % 2])
        if c == last and last % 2 == 1:
            prefetch_next_tile()
        accumulate(c, bufs[c % 2])
    if last % 2 == 0:
        prefetch_next_tile()

    o_t = acc_scr[...] / l_scr[...]
    for g in range(GROUP):
        o_ref[:, g * HEAD_DIM:(g + 1) * HEAD_DIM] = o_t[:, g * tq:(g + 1) * tq].T.astype(o_ref.dtype)


def _key_chunk(n_keys):
    fits = [t for t in range(LANES, ATTN_MAX_KEY_CHUNK + 1, LANES) if n_keys % t == 0]
    assert fits, n_keys
    return fits[-1]


def _attention(q, k_x, vt_x, k_c, vt_c):
    l = q.shape[0]
    n_ctx = k_c.shape[0]
    n_keys = l + n_ctx
    tq = min(ATTN_Q_ROWS, l)
    tk = _key_chunk(n_keys)
    n_chunks = n_keys // tk
    rem = l - (n_chunks - 1) * tk
    assert 0 <= rem and rem + n_ctx == tk and rem % LANES == 0
    gw = GROUP * HEAD_DIM
    m = GROUP * tq
    n_q = l // tq
    return pl.pallas_call(
        functools.partial(_attn_kernel, tk=tk, n_chunks=n_chunks),
        grid=(N_KV_HEADS, n_q),
        in_specs=[
            pl.BlockSpec((tq, gw), lambda h, i: (i, h)),
            pl.BlockSpec((tq, gw), lambda h, i: (jnp.minimum(i + 1, n_q - 1), h)),
            pl.BlockSpec((l, HEAD_DIM), lambda h, i: (0, h)),
            pl.BlockSpec((HEAD_DIM, l), lambda h, i: (h, 0)),
            pl.BlockSpec((n_ctx, HEAD_DIM), lambda h, i: (0, h)),
            pl.BlockSpec((HEAD_DIM, n_ctx), lambda h, i: (h, 0)),
        ],
        out_specs=pl.BlockSpec((tq, gw), lambda h, i: (i, h)),
        out_shape=jax.ShapeDtypeStruct((l, N_Q_HEADS * HEAD_DIM), BF16),
        scratch_shapes=[
            pltpu.VMEM((m, HEAD_DIM), BF16),
            pltpu.VMEM((m, HEAD_DIM), BF16),
            pltpu.VMEM((tk, m), F32),
            pltpu.VMEM((tk, m), F32),
            pltpu.VMEM((1, m), F32),
            pltpu.VMEM((1, m), F32),
            pltpu.VMEM((HEAD_DIM, m), F32),
            pltpu.VMEM((tk, HEAD_DIM), BF16),
            pltpu.VMEM((HEAD_DIM, tk), BF16),
        ],
        compiler_params=_params(("arbitrary", "arbitrary")),
        name="gqa_attention",
    )(q, q, k_x, vt_x, k_c, vt_c)


def _outproj_kernel(conv_ref, attn_ref, wc_ref, wa_ref, x_ref, gate_ref, o_ref):
    mix = _bdot(conv_ref[...], wc_ref) + _bdot(attn_ref[...], wa_ref)
    o_ref[...] = x_ref[...] + gate_ref[...] * mix


def _out_proj(conv_x, attn_x, w, x, gate):
    m, d = x.shape
    tm = min(PROJ_ROWS, m)
    tn = OUT_COLS
    kc = conv_x.shape[1]
    ka = attn_x.shape[1]
    assert kc == ka
    return pl.pallas_call(
        _outproj_kernel,
        grid=(m // tm, d // tn),
        in_specs=[
            pl.BlockSpec((tm, kc), lambda i, j: (i, 0)),
            pl.BlockSpec((tm, ka), lambda i, j: (i, 0)),
            pl.BlockSpec((kc, tn), lambda i, j: (0, j)),
            pl.BlockSpec((ka, tn), lambda i, j: (1, j)),
            pl.BlockSpec((tm, tn), lambda i, j: (i, j)),
            pl.BlockSpec((1, tn), lambda i, j: (0, j)),
        ],
        out_specs=pl.BlockSpec((tm, tn), lambda i, j: (i, j)),
        out_shape=jax.ShapeDtypeStruct((m, d), F32),
        compiler_params=_params(("arbitrary", "arbitrary")),
        name="out_proj",
    )(conv_x, attn_x, w, w, x, gate)


def _rope_tables(l):
    pos = jnp.arange(l, dtype=jnp.int32)
    row = (pos // GRID_W).astype(F32)
    col = (pos % GRID_W).astype(F32)
    axis_dim = HEAD_DIM // 2
    inv = ROPE_THETA ** (-jnp.arange(0, axis_dim, 2, dtype=F32) / axis_dim)
    ang = jnp.concatenate([row[:, None] * inv, col[:, None] * inv], axis=-1)
    cos, sin = jnp.cos(ang), jnp.sin(ang)
    cosf = jnp.repeat(cos, 2, axis=-1)
    sinf = jnp.stack([-sin, sin], axis=-1).reshape(l, HEAD_DIM)
    return cosf, sinf


def kernel(x, c, ctx, c_ctx, w_ada, b_ada, g_ffn1, w1_gate, w1_up, w1_down, g_mix, w_in, conv_w, conv_b,
           conv_ln_g, conv_ln_b, q_norm_g, k_norm_g, w_out, g_ffn2, w2_gate, w2_up, w2_down, g_final):
    b, l, d = x.shape
    depth = w_ada.shape[0]
    assert b == 1 and depth == 1 and l % GRID_W == 0
    n_ctx = ctx.shape[1]
    xs = x[0]
    cs = ctx[0]

    cc = jnp.zeros((SUBLANES, d), F32).at[0].set(c[0]).at[1].set(c_ctx)
    n_early = 5 * d
    b_ada_row = b_ada[0].reshape(1, -1)
    mods = _ada(cc, w_ada[0], b_ada_row, n_early)
    mx = mods[0].reshape(-1, d)
    mc = mods[1].reshape(-1, d)

    x1, hx, hc = _ffn(xs, mx[0:5], g_ffn1[0], w1_gate[0], w1_up[0], w1_down[0], g_mix=g_mix[0],
                      side=(cs, mc[0:5]))

    win = w_in[0]
    glu = _glu_proj(hx, win)
    q_gain = jnp.tile(q_norm_g[0] * (ATTN_SCALE * LOG2E), N_Q_HEADS).reshape(1, -1)
    k_gain = jnp.tile(k_norm_g[0], N_KV_HEADS).reshape(1, -1)
    cosf, sinf = _rope_tables(l)
    col_q = 2 * D_CONV
    col_k = col_q + N_Q_HEADS * HEAD_DIM
    q = _q_proj(hx, win, q_gain, cosf, sinf, col0=col_q)
    k_x, vt_x = _kv_proj(hx, win, k_gain, cosf, sinf, col0=col_k)
    k_c, vt_c = _kv_proj(hc, win, k_gain, jnp.ones((n_ctx, HEAD_DIM), F32), jnp.zeros((n_ctx, HEAD_DIM), F32),
                         col0=col_k)

    conv_x, mods_late = _conv_module(glu, conv_w[0], conv_b[0], conv_ln_g[0], conv_ln_b[0],
                                     cc, w_ada[0], b_ada_row, n_early)
    mx_late = mods_late[0].reshape(-1, d)
    attn_x = _attention(q, k_x, vt_x, k_c, vt_c)

    x2 = _out_proj(conv_x, attn_x, w_out[0], x1, mx_late[0:1])

    out = _ffn(x2, mx_late[1:4], g_ffn2[0], w2_gate[0], w2_up[0], w2_down[0], g_final=g_final)
    return out[None]
```

```python
import functools
import math

import jax
import jax.numpy as jnp
from jax import lax
from jax.experimental import pallas as pl
from jax.experimental.pallas import tpu as pltpu

F32 = jnp.float32
BF16 = jnp.bfloat16

EPS = 1e-6
HEAD_DIM = 128
N_Q_HEADS = 16
N_KV_HEADS = 4
GROUP = N_Q_HEADS // N_KV_HEADS
D_CONV = 2048
CONV_WIDTH = 31
GRID_W = 64
ROPE_THETA = 10000.0
LOG2E = math.log2(math.e)
ATTN_SCALE = 1.0 / math.sqrt(HEAD_DIM)

V7X_VMEM_BYTES = 64 * 1024 * 1024
VMEM_LIMIT_BYTES = V7X_VMEM_BYTES - 4 * 1024 * 1024
LANES = 128
SUBLANES = 8
MXU_WIDTH = 256

FFN_ROWS = 1024
FFN_COLS = MXU_WIDTH
FFN_ROW_CHUNK = 64
FFN_COL_BLOCK = 512
ADA_COLS = 512
PROJ_ROWS = 1024
PROJ_ROW_SPLITS = 4
GLU_COLS = 256
QKV_COLS = 512
OUT_COLS = 512
CONV_ROWS = 256
CONV_HALO = 16
CONV_ROW_BLOCK = 32
LN_ROW_BLOCK = 128
ATTN_Q_ROWS = 256
ATTN_ROW_SPLITS = 2
ATTN_MAX_KEY_CHUNK = 1408


def _params(semantics):
    return pltpu.CompilerParams(dimension_semantics=semantics, vmem_limit_bytes=VMEM_LIMIT_BYTES)


def _silu(v):
    return v * jax.nn.sigmoid(v)


def _bdot(a, w_ref):
    return jnp.dot(a, w_ref[...].astype(BF16), preferred_element_type=F32)


def _ada_kernel(c_ref, w_ref, b_ref, o_ref):
    a = _silu(c_ref[...]).astype(BF16)
    o_ref[...] = _bdot(a, w_ref) + b_ref[...]


def _ada(cc, w, b, n_cols):
    rows, d = cc.shape
    tn = ADA_COLS
    assert n_cols % tn == 0
    return pl.pallas_call(
        _ada_kernel,
        grid=(n_cols // tn,),
        in_specs=[
            pl.BlockSpec((rows, d), lambda j: (0, 0)),
            pl.BlockSpec((d, tn), lambda j: (0, j)),
            pl.BlockSpec((1, tn), lambda j: (0, j)),
        ],
        out_specs=pl.BlockSpec((rows, tn), lambda j: (0, j)),
        out_shape=jax.ShapeDtypeStruct((rows, n_cols), F32),
        compiler_params=_params(("arbitrary",)),
        name="ada_mod",
    )(cc, w, b)


class _RowGroup:
    def __init__(self, x_hbm, mod_ref, g_ref, gmix_ref, gfin_ref, out_hbm, hmix_hbm, scratch, *, scr0, hbm0, n_rows):
        self.x_hbm, self.mod_ref, self.g_ref, self.gmix_ref, self.gfin_ref = x_hbm, mod_ref, g_ref, gmix_ref, gfin_ref
        self.out_hbm, self.hmix_hbm = out_hbm, hmix_hbm
        self.hn_scr, self.acc_scr, self.xbuf, self.sem_x, self.sem_o, self.sem_h = scratch
        self.scr0, self.hbm0 = scr0, hbm0
        self.rc = FFN_ROW_CHUNK
        self.n_chunks = n_rows // self.rc
        assert self.n_chunks * self.rc == n_rows and self.n_chunks % 2 == 0
        self.d = self.acc_scr.shape[1]
        self.cw = FFN_COL_BLOCK
        self.col_blocks = [slice(c0, c0 + self.cw) for c0 in range(0, self.d, self.cw)]

    def scr_rows(self, r):
        return pl.ds(pl.multiple_of(self.scr0 + r * self.rc, self.rc), self.rc)

    def hbm_rows(self, r):
        return pl.ds(pl.multiple_of(self.hbm0 + r * self.rc, self.rc), self.rc)

    def x_copy(self, r, slot):
        return pltpu.make_async_copy(self.x_hbm.at[self.hbm_rows(r), :], self.xbuf.at[slot], self.sem_x.at[slot])

    def out_copy(self, r):
        return pltpu.make_async_copy(self.acc_scr.at[self.scr_rows(r), :], self.out_hbm.at[self.hbm_rows(r), :],
                                     self.sem_o.at[0])

    def hmix_copy(self, r):
        return pltpu.make_async_copy(self.hn_scr.at[self.scr_rows(r), :], self.hmix_hbm.at[self.hbm_rows(r), :],
                                     self.sem_h.at[0])

    def stream_x(self, process):
        self.x_copy(0, 0).start()

        def pair(j, carry):
            r = 2 * j
            self.x_copy(r + 1, 1).start()
            self.x_copy(r, 0).wait()
            process(r, 0)

            @pl.when(r + 2 < self.n_chunks)
            def _():
                self.x_copy(r + 2, 0).start()
            self.x_copy(r + 1, 1).wait()
            process(r + 1, 1)
            return carry
        lax.fori_loop(0, self.n_chunks // 2, pair, 0)

    def add_sumsq(self, part, v):
        sq = v * v
        for t in range(self.cw // LANES):
            part = part + sq[:, t * LANES:(t + 1) * LANES]
        return part

    def rstd_of(self, part):
        return lax.rsqrt(jnp.sum(part, axis=-1, keepdims=True) / self.d + EPS)

    def norm_mod(self, v, rstd, gain_ref, shift_row, scale_row, cols):
        y = v * rstd * gain_ref[:, cols]
        scale = self.mod_ref[scale_row:scale_row + 1, cols]
        return y * (1.0 + scale) + self.mod_ref[shift_row:shift_row + 1, cols]

    def prologue(self):
        def process(r, slot):
            rows = self.scr_rows(r)
            part = jnp.zeros((self.rc, LANES), F32)
            for cols in self.col_blocks:
                part = self.add_sumsq(part, self.xbuf[slot, :, cols])
            rstd = self.rstd_of(part)
            for cols in self.col_blocks:
                hn = self.norm_mod(self.xbuf[slot, :, cols], rstd, self.g_ref, 0, 1, cols)
                self.hn_scr[rows, cols] = hn.astype(BF16)
                self.acc_scr[rows, cols] = jnp.zeros((self.rc, self.cw), F32)
        self.stream_x(process)

    def epilogue(self):
        with_mix = self.hmix_hbm is not None
        write_out = self.out_hbm is not None

        def process(r, slot):
            rows = self.scr_rows(r)
            part = jnp.zeros((self.rc, LANES), F32)
            for cols in self.col_blocks:
                y = self.xbuf[slot, :, cols] + (0.5 * self.mod_ref[2:3, cols]) * self.acc_scr[rows, cols]
                self.acc_scr[rows, cols] = y
                part = self.add_sumsq(part, y)
            rstd = self.rstd_of(part)
            for cols in self.col_blocks:
                y = self.acc_scr[rows, cols]
                if with_mix:
                    self.hn_scr[rows, cols] = self.norm_mod(y, rstd, self.gmix_ref, 3, 4, cols).astype(BF16)
                if self.gfin_ref is not None:
                    self.acc_scr[rows, cols] = y * rstd * self.gfin_ref[:, cols]
            if with_mix:
                self.hmix_copy(r).start()
            if write_out:
                self.out_copy(r).start()
        self.stream_x(process)
        for _ in range(self.n_chunks):
            if with_mix:
                self.hmix_copy(0).wait()
            if write_out:
                self.out_copy(0).wait()


def _ffn_kernel(*refs, tm, n_side, n_f, with_mix, final_norm):
    refs = list(refs)
    x_hbm, mod_ref, g_ref, wg_ref, wu_ref, wd_ref = refs[:6]
    del refs[:6]
    gmix_ref = refs.pop(0) if with_mix else None
    gfin_ref = refs.pop(0) if final_norm else None
    xs_hbm, mods_ref = (refs.pop(0), refs.pop(0)) if n_side else (None, None)
    out_hbm = refs.pop(0)
    hmix_hbm = refs.pop(0) if with_mix else None
    hmixs_hbm = refs.pop(0) if n_side else None
    scratch = refs
    hn_scr, acc_scr = scratch[0], scratch[1]

    i = pl.program_id(0)
    f = pl.program_id(1)
    main = _RowGroup(x_hbm, mod_ref, g_ref, gmix_ref, gfin_ref, out_hbm, hmix_hbm, scratch,
                     scr0=0, hbm0=i * tm, n_rows=tm)
    side = _RowGroup(xs_hbm, mods_ref, g_ref, gmix_ref, None, None, hmixs_hbm, scratch,
                     scr0=tm, hbm0=0, n_rows=n_side) if n_side else None

    def swiglu(rows):
        hn = hn_scr[rows, :]
        gt = _bdot(hn, wg_ref)
        up = _bdot(hn, wu_ref)
        act = (_silu(gt) * up).astype(BF16)
        acc_scr[rows, :] += _bdot(act, wd_ref)

    @pl.when(f == 0)
    def _():
        main.prologue()

    if side is not None:
        @pl.when((f == 0) & (i == 0))
        def _():
            side.prologue()

    swiglu(slice(0, tm))

    if side is not None:
        @pl.when(i == 0)
        def _():
            swiglu(slice(tm, tm + n_side))

    @pl.when(f == n_f - 1)
    def _():
        main.epilogue()

    if side is not None:
        @pl.when((f == n_f - 1) & (i == 0))
        def _():
            side.epilogue()


def _ffn(x, mod, g, wg, wu, wd, *, g_mix=None, g_final=None, side=None):
    m, d = x.shape
    tm = min(FFN_ROWS, m)
    tf = FFN_COLS
    dff = wg.shape[1]
    assert m % tm == 0 and dff % tf == 0
    n_f = dff // tf
    with_mix = g_mix is not None
    final_norm = g_final is not None
    n_side = 0 if side is None else side[0].shape[0]
    assert not n_side or with_mix
    const = lambda i, f: (0, 0)
    hbm = pl.BlockSpec(memory_space=pl.ANY)
    in_specs = [
        hbm,
        pl.BlockSpec(mod.shape, const),
        pl.BlockSpec((1, d), const),
        pl.BlockSpec((d, tf), lambda i, f: (0, f)),
        pl.BlockSpec((d, tf), lambda i, f: (0, f)),
        pl.BlockSpec((tf, d), lambda i, f: (f, 0)),
    ]
    args = [x, mod, g.reshape(1, d), wg, wu, wd]
    if with_mix:
        in_specs.append(pl.BlockSpec((1, d), const))
        args.append(g_mix.reshape(1, d))
    if final_norm:
        in_specs.append(pl.BlockSpec((1, d), const))
        args.append(g_final.reshape(1, d))
    if n_side:
        in_specs += [hbm, pl.BlockSpec(side[1].shape, const)]
        args += [side[0], side[1]]
    out_shape = [jax.ShapeDtypeStruct((m, d), F32)]
    if with_mix:
        out_shape.append(jax.ShapeDtypeStruct((m, d), BF16))
    if n_side:
        out_shape.append(jax.ShapeDtypeStruct((n_side, d), BF16))
    res = pl.pallas_call(
        functools.partial(_ffn_kernel, tm=tm, n_side=n_side, n_f=n_f, with_mix=with_mix, final_norm=final_norm),
        grid=(m // tm, n_f),
        in_specs=in_specs,
        out_specs=[hbm] * len(out_shape),
        out_shape=out_shape,
        scratch_shapes=[
            pltpu.VMEM((tm + n_side, d), BF16),
            pltpu.VMEM((tm + n_side, d), F32),
            pltpu.VMEM((2, FFN_ROW_CHUNK, d), F32),
            pltpu.SemaphoreType.DMA((2,)),
            pltpu.SemaphoreType.DMA((1,)),
            pltpu.SemaphoreType.DMA((1,)),
        ],
        compiler_params=_params(("arbitrary", "arbitrary")),
        name="ffn_mix" if with_mix else "ffn_final",
    )(*args)
    return res if len(res) > 1 else res[0]


def _glu_kernel(a_ref, wu_ref, wg_ref, o_ref):
    a = a_ref[...]
    u = _bdot(a, wu_ref)
    g = _bdot(a, wg_ref)
    o_ref[...] = u * jax.nn.sigmoid(g)


def _glu_proj(a, w):
    m, d = a.shape
    tm = min(PROJ_ROWS, m)
    tn = GLU_COLS
    nb = D_CONV // tn
    return pl.pallas_call(
        _glu_kernel,
        grid=(m // tm, nb),
        in_specs=[
            pl.BlockSpec((tm, d), lambda i, j: (i, 0)),
            pl.BlockSpec((d, tn), lambda i, j: (0, j)),
            pl.BlockSpec((d, tn), lambda i, j: (0, j + nb)),
        ],
        out_specs=pl.BlockSpec((tm, tn), lambda i, j: (i, j)),
        out_shape=jax.ShapeDtypeStruct((m, D_CONV), F32),
        compiler_params=_params(("arbitrary", "arbitrary")),
        name="glu_proj",
    )(a, w, w)


def _project_norm_rope(a_ref, w_ref, gain_ref, cos_ref, sin_ref, o_ref):
    tm = a_ref.shape[0]
    part = tm // PROJ_ROW_SPLITS
    for s in range(PROJ_ROW_SPLITS):
        rows = slice(s * part, (s + 1) * part)
        p = _bdot(a_ref[rows, :], w_ref)
        cosf = cos_ref[rows, :]
        sinf = sin_ref[rows, :]
        even = (lax.broadcasted_iota(jnp.int32, cosf.shape, 1) % 2) == 0
        for h in range(p.shape[1] // HEAD_DIM):
            cols = slice(h * HEAD_DIM, (h + 1) * HEAD_DIM)
            ph = p[:, cols]
            ms = jnp.mean(ph * ph, axis=-1, keepdims=True)
            y = ph * lax.rsqrt(ms + EPS) * gain_ref[:, cols]
            partner = jnp.where(even, pltpu.roll(y, HEAD_DIM - 1, 1), pltpu.roll(y, 1, 1))
            o_ref[rows, cols] = (y * cosf + partner * sinf).astype(o_ref.dtype)


def _q_kernel(a_ref, w_ref, gain_ref, cos_ref, sin_ref, o_ref):
    _project_norm_rope(a_ref, w_ref, gain_ref, cos_ref, sin_ref, o_ref)


def _kv_kernel(a_ref, w_ref, gain_ref, cos_ref, sin_ref, k_ref, vt_ref):
    j = pl.program_id(0)

    @pl.when(j == 0)
    def _():
        _project_norm_rope(a_ref, w_ref, gain_ref, cos_ref, sin_ref, k_ref)

    @pl.when(j == 1)
    def _():
        tm = a_ref.shape[0]
        part = tm // PROJ_ROW_SPLITS
        for s in range(PROJ_ROW_SPLITS):
            rows = slice(s * part, (s + 1) * part)
            vt_ref[:, rows] = _bdot(a_ref[rows, :], w_ref).T.astype(vt_ref.dtype)


def _q_proj(a, w, gain, cosf, sinf, *, col0):
    m, d = a.shape
    tm = min(PROJ_ROWS, m)
    tn = QKV_COLS
    n = N_Q_HEADS * HEAD_DIM
    return pl.pallas_call(
        _q_kernel,
        grid=(m // tm, n // tn),
        in_specs=[
            pl.BlockSpec((tm, d), lambda i, j: (i, 0)),
            pl.BlockSpec((d, tn), lambda i, j: (0, j + col0 // tn)),
            pl.BlockSpec((1, tn), lambda i, j: (0, j)),
            pl.BlockSpec((tm, HEAD_DIM), lambda i, j: (i, 0)),
            pl.BlockSpec((tm, HEAD_DIM), lambda i, j: (i, 0)),
        ],
        out_specs=pl.BlockSpec((tm, tn), lambda i, j: (i, j)),
        out_shape=jax.ShapeDtypeStruct((m, n), BF16),
        compiler_params=_params(("arbitrary", "arbitrary")),
        name="q_proj",
    )(a, w, gain, cosf, sinf)


def _kv_proj(a, w, gain, cosf, sinf, *, col0):
    m, d = a.shape
    tm = min(PROJ_ROWS, m)
    n = N_KV_HEADS * HEAD_DIM
    assert n == QKV_COLS
    n_i = m // tm
    return pl.pallas_call(
        _kv_kernel,
        grid=(2, n_i),
        in_specs=[
            pl.BlockSpec((tm, d), lambda j, i: (i, 0)),
            pl.BlockSpec((d, n), lambda j, i: (0, j + col0 // n)),
            pl.BlockSpec((1, n), lambda j, i: (0, 0)),
            pl.BlockSpec((tm, HEAD_DIM), lambda j, i: (i, 0)),
            pl.BlockSpec((tm, HEAD_DIM), lambda j, i: (i, 0)),
        ],
        out_specs=[
            pl.BlockSpec((tm, n), lambda j, i: (i * (1 - j) + (n_i - 1) * j, 0)),
            pl.BlockSpec((n, tm), lambda j, i: (0, i * j)),
        ],
        out_shape=[jax.ShapeDtypeStruct((m, n), BF16), jax.ShapeDtypeStruct((n, m), BF16)],
        compiler_params=_params(("arbitrary", "arbitrary")),
        name="kv_proj",
    )(a, w, gain, cosf, sinf)


def _conv_kernel(prev_ref, main_ref, next_ref, w_ref, b_ref, lng_ref, lnb_ref, c_ref, wada_ref, bada_ref,
                 o_ref, mod_ref, e_scr, ph_scr, y_scr):
    _ada_kernel(c_ref, wada_ref, bada_ref, mod_ref)
    i = pl.program_id(0)
    n = pl.num_programs(0)
    ts, ch = main_ref.shape
    e_scr[0:CONV_HALO, :] = jnp.where(i > 0, prev_ref[...], 0.0)
    e_scr[CONV_HALO:CONV_HALO + ts, :] = main_ref[...]
    e_scr[CONV_HALO + ts:CONV_HALO + ts + CONV_HALO, :] = jnp.where(i < n - 1, next_ref[...], 0.0)
    first = CONV_HALO - CONV_WIDTH // 2
    ph_rows = ph_scr.shape[1]

    def lane_tile(c, carry):
        cols = pl.ds(pl.multiple_of(c * LANES, LANES), LANES)
        for ph in range(SUBLANES):
            ph_scr[ph] = e_scr[pl.ds(ph, ph_rows), cols]
        taps = [w_ref[k:k + 1, cols] for k in range(CONV_WIDTH)]
        bias = b_ref[:, cols]
        for r in range(ts // CONV_ROW_BLOCK):
            r0 = r * CONV_ROW_BLOCK
            acc = jnp.zeros((CONV_ROW_BLOCK, LANES), F32)
            for k in range(CONV_WIDTH):
                off = first + k
                acc = acc + ph_scr[off % SUBLANES, pl.ds(r0 + off - off % SUBLANES, CONV_ROW_BLOCK), :] * taps[k]
            y_scr[pl.ds(r0, CONV_ROW_BLOCK), cols] = acc + bias
        return carry
    lax.fori_loop(0, ch // LANES, lane_tile, 0)

    lb = min(LN_ROW_BLOCK, ts)

    def ln_rows(r, carry):
        rows = pl.ds(pl.multiple_of(r * lb, lb), lb)
        y = y_scr[rows, :]
        mu = jnp.mean(y, axis=-1, keepdims=True)
        yc = y - mu
        var = jnp.mean(yc * yc, axis=-1, keepdims=True)
        z = yc * lax.rsqrt(var + EPS) * lng_ref[...] + lnb_ref[...]
        o_ref[rows, :] = _silu(z).astype(o_ref.dtype)
        return carry
    lax.fori_loop(0, ts // lb, ln_rows, 0)


def _conv_module(glu, w, b, ln_g, ln_b, cc, w_ada, b_ada, ada_col0):
    l, ch = glu.shape
    ts = min(CONV_ROWS, l)
    hb = ts // CONV_HALO
    n = l // ts
    last_halo_block = l // CONV_HALO - 1
    mod_rows, d = cc.shape
    n_late = w_ada.shape[1] - ada_col0
    tn = n_late // n
    assert tn * n == n_late and tn % LANES == 0 and ada_col0 % tn == 0
    ada_blk0 = ada_col0 // tn
    ph_rows = ts + 2 * CONV_HALO - SUBLANES
    assert CONV_HALO - CONV_WIDTH // 2 + CONV_WIDTH - 1 + ts <= ph_rows + SUBLANES - 1
    return pl.pallas_call(
        _conv_kernel,
        grid=(n,),
        in_specs=[
            pl.BlockSpec((CONV_HALO, ch), lambda i: (jnp.maximum(i * hb - 1, 0), 0)),
            pl.BlockSpec((ts, ch), lambda i: (i, 0)),
            pl.BlockSpec((CONV_HALO, ch), lambda i: (jnp.minimum((i + 1) * hb, last_halo_block), 0)),
            pl.BlockSpec((CONV_WIDTH, ch), lambda i: (0, 0)),
            pl.BlockSpec((1, ch), lambda i: (0, 0)),
            pl.BlockSpec((1, ch), lambda i: (0, 0)),
            pl.BlockSpec((1, ch), lambda i: (0, 0)),
            pl.BlockSpec((mod_rows, d), lambda i: (0, 0)),
            pl.BlockSpec((d, tn), lambda i: (0, i + ada_blk0)),
            pl.BlockSpec((1, tn), lambda i: (0, i + ada_blk0)),
        ],
        out_specs=[
            pl.BlockSpec((ts, ch), lambda i: (i, 0)),
            pl.BlockSpec((mod_rows, tn), lambda i: (0, i)),
        ],
        out_shape=[
            jax.ShapeDtypeStruct((l, ch), BF16),
            jax.ShapeDtypeStruct((mod_rows, n_late), F32),
        ],
        scratch_shapes=[
            pltpu.VMEM((ts + 2 * CONV_HALO, ch), F32),
            pltpu.VMEM((SUBLANES, ph_rows, LANES), F32),
            pltpu.VMEM((ts, ch), F32),
        ],
        compiler_params=_params(("arbitrary",)),
        name="conv_module",
    )(glu, glu, glu, w, b.reshape(1, ch), ln_g.reshape(1, ch), ln_b.reshape(1, ch), cc, w_ada, b_ada)


def _attn_kernel(q_ref, qn_ref, kx_ref, vtx_ref, kc_ref, vtc_ref, o_ref, qs_scr, qsn_scr, s0_scr, s1_scr,
                 m_scr, l_scr, acc_scr, klast_scr, vtlast_scr, *, tk, n_chunks):
    tq = q_ref.shape[0]
    part = GROUP * tq // ATTN_ROW_SPLITS
    last = n_chunks - 1
    i = pl.program_id(1)

    def keys(c):
        return pl.ds(pl.multiple_of(c * tk, tk), tk)

    def is_last(c):
        return isinstance(c, int) and c == last

    def stack_heads(src_ref, dst_scr):
        for g in range(GROUP):
            dst_scr[g * tq:(g + 1) * tq, :] = src_ref[:, g * HEAD_DIM:(g + 1) * HEAD_DIM]

    def scores(c, s_scr, q_scr=qs_scr):
        k = klast_scr[...] if is_last(c) else kx_ref[keys(c), :]
        for h in range(ATTN_ROW_SPLITS):
            r = slice(h * part, (h + 1) * part)
            s_scr[:, r] = lax.dot_general(k, q_scr[r, :], (((1,), (1,)), ((), ())),
                                          preferred_element_type=F32)

    @pl.when(i == 0)
    def _():
        n_lat = kx_ref.shape[0]
        rem = n_lat - last * tk
        if rem:
            klast_scr[0:rem, :] = kx_ref[n_lat - rem:n_lat, :]
            vtlast_scr[:, 0:rem] = vtx_ref[:, n_lat - rem:n_lat]
        klast_scr[rem:tk, :] = kc_ref[...]
        vtlast_scr[:, rem:tk] = vtc_ref[...]
        stack_heads(q_ref, qs_scr)
        scores(0, s0_scr)

    @pl.when(i > 0)
    def _():
        qs_scr[...] = qsn_scr[...]

    m_scr[...] = jnp.full(m_scr.shape, -jnp.inf, F32)
    l_scr[...] = jnp.zeros(l_scr.shape, F32)
    acc_scr[...] = jnp.zeros(acc_scr.shape, F32)

    def prefetch_next_tile():
        stack_heads(qn_ref, qsn_scr)
        scores(0, s0_scr, qsn_scr)

    def accumulate(c, s_scr):
        vt = vtlast_scr[...] if is_last(c) else vtx_ref[:, keys(c)]
        for h in range(ATTN_ROW_SPLITS):
            r = slice(h * part, (h + 1) * part)
            s = s_scr[:, r]
            m_prev = m_scr[:, r]
            m_new = jnp.maximum(m_prev, jnp.max(s, axis=0, keepdims=True))
            alpha = jnp.exp2(m_prev - m_new)
            p = jnp.exp2(s - m_new)
            l_scr[:, r] = alpha * l_scr[:, r] + jnp.sum(p, axis=0, keepdims=True)
            acc_scr[:, r] = alpha * acc_scr[:, r] + jnp.dot(vt, p.astype(BF16), preferred_element_type=F32)
            m_scr[:, r] = m_new

    bufs = (s0_scr, s1_scr)
    n_pairs = max((n_chunks - 2) // 2, 0)

    def pair(j, carry):
        scores(2 * j + 1, s1_scr)
        accumulate(2 * j, s0_scr)
        scores(2 * j + 2, s0_scr)
        accumulate(2 * j + 1, s1_scr)
        return carry
    lax.fori_loop(0, n_pairs, pair, 0)
    for c in range(2 * n_pairs, n_chunks):
        if c + 1 < n_chunks:
            scores(c + 1, bufs[(c + 1) % 2])
        if c == last and last % 2 == 1:
            prefetch_next_tile()
        accumulate(c, bufs[c % 2])
    if last % 2 == 0:
        prefetch_next_tile()

    o_t = acc_scr[...] / l_scr[...]
    for g in range(GROUP):
        o_ref[:, g * HEAD_DIM:(g + 1) * HEAD_DIM] = o_t[:, g * tq:(g + 1) * tq].T.astype(o_ref.dtype)


def _key_chunk(n_keys):
    fits = [t for t in range(LANES, ATTN_MAX_KEY_CHUNK + 1, LANES) if n_keys % t == 0]
    assert fits, n_keys
    return fits[-1]


def _attention(q, k_x, vt_x, k_c, vt_c):
    l = q.shape[0]
    n_ctx = k_c.shape[0]
    n_keys = l + n_ctx
    tq = min(ATTN_Q_ROWS, l)
    tk = _key_chunk(n_keys)
    n_chunks = n_keys // tk
    rem = l - (n_chunks - 1) * tk
    assert 0 <= rem and rem + n_ctx == tk and rem % LANES == 0
    gw = GROUP * HEAD_DIM
    m = GROUP * tq
    n_q = l // tq
    return pl.pallas_call(
        functools.partial(_attn_kernel, tk=tk, n_chunks=n_chunks),
        grid=(N_KV_HEADS, n_q),
        in_specs=[
            pl.BlockSpec((tq, gw), lambda h, i: (i, h)),
            pl.BlockSpec((tq, gw), lambda h, i: (jnp.minimum(i + 1, n_q - 1), h)),
            pl.BlockSpec((l, HEAD_DIM), lambda h, i: (0, h)),
            pl.BlockSpec((HEAD_DIM, l), lambda h, i: (h, 0)),
            pl.BlockSpec((n_ctx, HEAD_DIM), lambda h, i: (0, h)),
            pl.BlockSpec((HEAD_DIM, n_ctx), lambda h, i: (h, 0)),
        ],
        out_specs=pl.BlockSpec((tq, gw), lambda h, i: (i, h)),
        out_shape=jax.ShapeDtypeStruct((l, N_Q_HEADS * HEAD_DIM), BF16),
        scratch_shapes=[
            pltpu.VMEM((m, HEAD_DIM), BF16),
            pltpu.VMEM((m, HEAD_DIM), BF16),
            pltpu.VMEM((tk, m), F32),
            pltpu.VMEM((tk, m), F32),
            pltpu.VMEM((1, m), F32),
            pltpu.VMEM((1, m), F32),
            pltpu.VMEM((HEAD_DIM, m), F32),
            pltpu.VMEM((tk, HEAD_DIM), BF16),
            pltpu.VMEM((HEAD_DIM, tk), BF16),
        ],
        compiler_params=_params(("arbitrary", "arbitrary")),
        name="gqa_attention",
    )(q, q, k_x, vt_x, k_c, vt_c)


def _outproj_kernel(conv_ref, attn_ref, wc_ref, wa_ref, x_ref, gate_ref, o_ref):
    mix = _bdot(conv_ref[...], wc_ref) + _bdot(attn_ref[...], wa_ref)
    o_ref[...] = x_ref[...] + gate_ref[...] * mix


def _out_proj(conv_x, attn_x, w, x, gate):
    m, d = x.shape
    tm = min(PROJ_ROWS, m)
    tn = OUT_COLS
    kc = conv_x.shape[1]
    ka = attn_x.shape[1]
    assert kc == ka
    return pl.pallas_call(
        _outproj_kernel,
        grid=(m // tm, d // tn),
        in_specs=[
            pl.BlockSpec((tm, kc), lambda i, j: (i, 0)),
            pl.BlockSpec((tm, ka), lambda i, j: (i, 0)),
            pl.BlockSpec((kc, tn), lambda i, j: (0, j)),
            pl.BlockSpec((ka, tn), lambda i, j: (1, j)),
            pl.BlockSpec((tm, tn), lambda i, j: (i, j)),
            pl.BlockSpec((1, tn), lambda i, j: (0, j)),
        ],
        out_specs=pl.BlockSpec((tm, tn), lambda i, j: (i, j)),
        out_shape=jax.ShapeDtypeStruct((m, d), F32),
        compiler_params=_params(("arbitrary", "arbitrary")),
        name="out_proj",
    )(conv_x, attn_x, w, w, x, gate)


def _rope_tables(l):
    pos = jnp.arange(l, dtype=jnp.int32)
    row = (pos // GRID_W).astype(F32)
    col = (pos % GRID_W).astype(F32)
    axis_dim = HEAD_DIM // 2
    inv = ROPE_THETA ** (-jnp.arange(0, axis_dim, 2, dtype=F32) / axis_dim)
    ang = jnp.concatenate([row[:, None] * inv, col[:, None] * inv], axis=-1)
    cos, sin = jnp.cos(ang), jnp.sin(ang)
    cosf = jnp.repeat(cos, 2, axis=-1)
    sinf = jnp.stack([-sin, sin], axis=-1).reshape(l, HEAD_DIM)
    return cosf, sinf


def kernel(x, c, ctx, c_ctx, w_ada, b_ada, g_ffn1, w1_gate, w1_up, w1_down, g_mix, w_in, conv_w, conv_b,
           conv_ln_g, conv_ln_b, q_norm_g, k_norm_g, w_out, g_ffn2, w2_gate, w2_up, w2_down, g_final):
    b, l, d = x.shape
    depth = w_ada.shape[0]
    assert b == 1 and depth == 1 and l % GRID_W == 0
    n_ctx = ctx.shape[1]
    xs = x[0]
    cs = ctx[0]

    cc = jnp.zeros((SUBLANES, d), F32).at[0].set(c[0]).at[1].set(c_ctx)
    n_early = 5 * d
    b_ada_row = b_ada[0].reshape(1, -1)
    mods = _ada(cc, w_ada[0], b_ada_row, n_early)
    mx = mods[0].reshape(-1, d)
    mc = mods[1].reshape(-1, d)

    x1, hx, hc = _ffn(xs, mx[0:5], g_ffn1[0], w1_gate[0], w1_up[0], w1_down[0], g_mix=g_mix[0],
                      side=(cs, mc[0:5]))

    win = w_in[0]
    glu = _glu_proj(hx, win)
    q_gain = jnp.tile(q_norm_g[0] * (ATTN_SCALE * LOG2E), N_Q_HEADS).reshape(1, -1)
    k_gain = jnp.tile(k_norm_g[0], N_KV_HEADS).reshape(1, -1)
    cosf, sinf = _rope_tables(l)
    col_q = 2 * D_CONV
    col_k = col_q + N_Q_HEADS * HEAD_DIM
    q = _q_proj(hx, win, q_gain, cosf, sinf, col0=col_q)
    k_x, vt_x = _kv_proj(hx, win, k_gain, cosf, sinf, col0=col_k)
    k_c, vt_c = _kv_proj(hc, win, k_gain, jnp.ones((n_ctx, HEAD_DIM), F32), jnp.zeros((n_ctx, HEAD_DIM), F32),
                         col0=col_k)

    conv_x, mods_late = _conv_module(glu, conv_w[0], conv_b[0], conv_ln_g[0], conv_ln_b[0],
                                     cc, w_ada[0], b_ada_row, n_early)
    mx_late = mods_late[0].reshape(-1, d)
    attn_x = _attention(q, k_x, vt_x, k_c, vt_c)

    x2 = _out_proj(conv_x, attn_x, w_out[0], x1, mx_late[0:1])

    out = _ffn(x2, mx_late[1:4], g_ffn2[0], w2_gate[0], w2_up[0], w2_down[0], g_final=g_final)
    return out[None]
```
